```python
import math
import jax
import jax.numpy as jnp
from jax import lax
import numpy as np

D_MODEL = 1024
BATCH = 8
SEQ = 4096
DEPTH = 1

HEAD_DIM = 64
DIL_PAIRS = ((128, 1), (512, 4), (2048, 16))
DIL_HEADS_PER_GROUP = 4
DIL_HEADS = DIL_HEADS_PER_GROUP * len(DIL_PAIRS)
DIL_WIDTH = DIL_HEADS * HEAD_DIM
DIL_OUT = DIL_HEADS_PER_GROUP * HEAD_DIM
DIL_QBLOCK = 64
NA_HEADS = 8
NA_WIDTH = NA_HEADS * HEAD_DIM
GRID_W = 64
NA_KH_MAX = 8
NA_KW = 16
NA_QC = 16
NA_KC = NA_QC + NA_KW
N_EXPERTS = 16
EC_CAPACITY_FACTOR = 2
EXPERT_FF = 1024
ROPE_THETA = 10000.0
EPS = 1e-6
SPLIT_SIZES = (DIL_WIDTH, DIL_WIDTH, DIL_WIDTH, NA_WIDTH, NA_WIDTH, NA_WIDTH, D_MODEL, D_MODEL)
IN_COLS = sum(SPLIT_SIZES)
SPLIT_POINTS = tuple(sum(SPLIT_SIZES[:i + 1]) for i in range(len(SPLIT_SIZES) - 1))

kernel_name = 'hybrid_dilated_neighbourhood_ec_block'


def rms_norm(t, g):
    tf = t.astype(jnp.float32)
    y = tf * lax.rsqrt(jnp.mean(tf * tf, axis=-1, keepdims=True) + EPS)
    return (y * g.astype(jnp.float32)).astype(t.dtype)


def rotary(t, pos):
    half = t.shape[-1] // 2
    inv = ROPE_THETA ** (-jnp.arange(half, dtype=jnp.float32) / half)
    ang = pos[:, None] * inv[None, :]
    cos = jnp.cos(ang)[None, :, None, :]
    sin = jnp.sin(ang)[None, :, None, :]
    tf = t.astype(jnp.float32)
    t1, t2 = tf[..., :half], tf[..., half:]
    return jnp.concatenate([t1 * cos - t2 * sin, t2 * cos + t1 * sin], axis=-1).astype(t.dtype)


def dilated_group_attention(q, k, v, window, dilation):
    B, S, H, hd = q.shape
    d = dilation
    R = (window // 2) // d
    L = S // d
    qb = math.gcd(L, DIL_QBLOCK)
    nb = L // qb
    kw = qb + 2 * R

    def to_sub(t):
        return t.reshape(B, L, d, H, hd).transpose(0, 2, 3, 1, 4)

    pad = ((0, 0), (0, 0), (0, 0), (R, R), (0, 0))
    qs = to_sub(q).reshape(B, d, H, nb, qb, hd)
    ks = jnp.pad(to_sub(k), pad)
    vs = jnp.pad(to_sub(v), pad)
    kidx = jnp.arange(nb)[:, None] * qb + jnp.arange(kw)[None, :]
    kb = ks[:, :, :, kidx]
    vb = vs[:, :, :, kidx]
    s = jnp.einsum('brhnqc,brhnkc->brhnqk', qs, kb).astype(jnp.float32) * (hd ** -0.5)
    u_q = jnp.arange(nb)[:, None] * qb + jnp.arange(qb)[None, :]
    u_k = kidx - R
    off = u_k[:, None, :] - u_q[:, :, None]
    valid = (jnp.abs(off) <= R) & (u_k[:, None, :] >= 0) & (u_k[:, None, :] < L)
    s = jnp.where(valid, s, -jnp.inf)
    m = jnp.max(s, axis=-1, keepdims=True)
    p = jnp.exp(s - m)
    den = jnp.sum(p, axis=-1, keepdims=True)
    out = jnp.einsum('brhnqk,brhnkc->brhnqc', p, vb.astype(jnp.float32)) / den
    lse = (m + jnp.log(den))[..., 0]
    out = out.reshape(B, d, H, L, hd).transpose(0, 3, 1, 2, 4).reshape(B, S, H, hd)
    lse = lse.reshape(B, d, H, L).transpose(0, 3, 1, 2).reshape(B, S, H)
    return out.astype(q.dtype), lse


def neighbourhood_attention(q, k, v, rpb):
    B, S, H, hd = q.shape
    rows = S // GRID_W
    kh = min(NA_KH_MAX, rows)
    ncb = GRID_W // NA_QC
    qg = q.reshape(B, rows, GRID_W, H, hd)
    kg = k.reshape(B, rows, GRID_W, H, hd)
    vg = v.reshape(B, rows, GRID_W, H, hd)
    qcol = jnp.arange(GRID_W).reshape(ncb, NA_QC)
    kc_start = jnp.clip(jnp.arange(ncb) * NA_QC - NA_KW // 2, 0, GRID_W - NA_KC)
    kcol = kc_start[:, None] + jnp.arange(NA_KC)[None, :]
    ws = jnp.clip(qcol - NA_KW // 2, 0, GRID_W - NA_KW)
    col_ok = (kcol[:, None, :] >= ws[:, :, None]) & (kcol[:, None, :] < ws[:, :, None] + NA_KW)
    dc_idx = jnp.clip(kcol[:, None, :] - qcol[:, :, None] + NA_KW - 1, 0, 2 * NA_KW - 2)
    scale = hd ** -0.5

    def row_block(r):
        rs = jnp.clip(r - kh // 2, 0, rows - kh)
        q_r = lax.dynamic_index_in_dim(qg, r, axis=1, keepdims=False)
        k_r = lax.dynamic_slice_in_dim(kg, rs, kh, axis=1)
        v_r = lax.dynamic_slice_in_dim(vg, rs, kh, axis=1)
        kb = k_r[:, :, kcol]
        vb = v_r[:, :, kcol]
        qb = q_r.reshape(B, ncb, NA_QC, H, hd)
        s = jnp.einsum('bnqhc,bjnkhc->bhnqjk', qb, kb).astype(jnp.float32) * scale
        dr_idx = rs + jnp.arange(kh) - r + NA_KH_MAX - 1
        bias = rpb[:, dr_idx[:, None, None, None], dc_idx[None]]
        s = s + bias.transpose(0, 2, 3, 1, 4).astype(jnp.float32)[None]
        s = jnp.where(col_ok[:, :, None, :], s, -jnp.inf)
        p = jax.nn.softmax(s.reshape(B, H, ncb, NA_QC, kh * NA_KC), axis=-1)
        p = p.reshape(B, H, ncb, NA_QC, kh, NA_KC)
        o = jnp.einsum('bhnqjk,bjnkhc->bnqhc', p, vb.astype(jnp.float32))
        return o.reshape(B, GRID_W, H, hd).astype(q.dtype)

    out = lax.map(row_block, jnp.arange(rows))
    return out.transpose(1, 0, 2, 3, 4).reshape(B, S, H, hd)


def expert_choice_ffn(h, w_router, w_gate, w_up, w_down):
    B, S, D = h.shape
    cap = (EC_CAPACITY_FACTOR * S) // N_EXPERTS
    logits = jnp.einsum('bsd,de->bse', h, w_router).astype(jnp.float32)
    aff = jax.nn.softmax(logits, axis=-1)
    gates, idx = lax.top_k(aff.transpose(0, 2, 1), cap)
    flat = jnp.arange(B)[:, None, None] * S + idx
    h_flat = h.reshape(B * S, D)
    xin = h_flat[flat]
    a = jnp.einsum('becd,edf->becf', xin, w_gate)
    u = jnp.einsum('becd,edf->becf', xin, w_up)
    y = jnp.einsum('becf,efd->becd', jax.nn.silu(a) * u, w_down)
    y = y * gates[..., None].astype(y.dtype)
    out = jnp.zeros((B * S, D), y.dtype).at[flat.reshape(-1)].add(y.reshape(-1, D))
    return out.reshape(B, S, D)


def setup_inputs(seed: int = 0) -> dict:
    key = jax.random.key(seed)
    ks = jax.random.split(key, 16)
    f32 = jnp.float32

    def nrm(k, shape, scale):
        return jax.random.normal(k, shape, f32) * scale

    def gain(k, shape):
        return 1.0 + 0.05 * jax.random.normal(k, shape, f32)

    L = DEPTH
    return {
        'x': nrm(ks[0], (BATCH, SEQ, D_MODEL), 1.0),
        'norm1_g': gain(ks[1], (L, D_MODEL)),
        'w_in': nrm(ks[2], (L, D_MODEL, IN_COLS), D_MODEL ** -0.5),
        'dil_q_norm_g': gain(ks[3], (L, HEAD_DIM)),
        'dil_k_norm_g': gain(ks[4], (L, HEAD_DIM)),
        'na_q_norm_g': gain(ks[5], (L, HEAD_DIM)),
        'na_k_norm_g': gain(ks[6], (L, HEAD_DIM)),
        'na_rpb': nrm(ks[7], (L, NA_HEADS, 2 * NA_KH_MAX - 1, 2 * NA_KW - 1), 0.1),
        'w_dil_branch': nrm(ks[8], (L, DIL_OUT, D_MODEL), DIL_OUT ** -0.5),
        'w_na_branch': nrm(ks[9], (L, NA_WIDTH, D_MODEL), NA_WIDTH ** -0.5),
        'w_out': nrm(ks[10], (L, D_MODEL, D_MODEL), D_MODEL ** -0.5),
        'norm2_g': gain(ks[11], (L, D_MODEL)),
        'w_router': nrm(ks[12], (L, D_MODEL, N_EXPERTS), D_MODEL ** -0.5),
        'w_gate': nrm(ks[13], (L, N_EXPERTS, D_MODEL, EXPERT_FF), D_MODEL ** -0.5),
        'w_up': nrm(ks[14], (L, N_EXPERTS, D_MODEL, EXPERT_FF), D_MODEL ** -0.5),
        'w_down': nrm(ks[15], (L, N_EXPERTS, EXPERT_FF, D_MODEL), EXPERT_FF ** -0.5),
    }


def reference(x, norm1_g, w_in, dil_q_norm_g, dil_k_norm_g, na_q_norm_g, na_k_norm_g, na_rpb,
              w_dil_branch, w_na_branch, w_out, norm2_g, w_router, w_gate, w_up, w_down):
    B, S, _ = x.shape
    pos = jnp.arange(S, dtype=jnp.float32)
    for l in range(DEPTH):
        h = rms_norm(x, norm1_g[l])
        proj = jnp.einsum('bsd,dc->bsc', h, w_in[l])
        qa, ka, va, qn, kn, vn, ga, gn = jnp.split(proj, SPLIT_POINTS, axis=-1)

        qa = rotary(rms_norm(qa.reshape(B, S, DIL_HEADS, HEAD_DIM), dil_q_norm_g[l]), pos)
        ka = rotary(rms_norm(ka.reshape(B, S, DIL_HEADS, HEAD_DIM), dil_k_norm_g[l]), pos)
        va = va.reshape(B, S, DIL_HEADS, HEAD_DIM)
        outs = []
        lses = []
        for g, (window, dilation) in enumerate(DIL_PAIRS):
            hs = slice(g * DIL_HEADS_PER_GROUP, (g + 1) * DIL_HEADS_PER_GROUP)
            o_g, lse_g = dilated_group_attention(qa[:, :, hs], ka[:, :, hs], va[:, :, hs], window, dilation)
            outs.append(o_g.astype(jnp.float32))
            lses.append(lse_g)
        wts = jax.nn.softmax(jnp.stack(lses, axis=0), axis=0)
        y_a = jnp.sum(wts[..., None] * jnp.stack(outs, axis=0), axis=0).astype(x.dtype)
        y_a = jnp.einsum('bsc,cd->bsd', y_a.reshape(B, S, DIL_OUT), w_dil_branch[l])

        qn = rms_norm(qn.reshape(B, S, NA_HEADS, HEAD_DIM), na_q_norm_g[l])
        kn = rms_norm(kn.reshape(B, S, NA_HEADS, HEAD_DIM), na_k_norm_g[l])
        vn = vn.reshape(B, S, NA_HEADS, HEAD_DIM)
        y_b = neighbourhood_attention(qn, kn, vn, na_rpb[l])
        y_b = jnp.einsum('bsc,cd->bsd', y_b.reshape(B, S, NA_WIDTH), w_na_branch[l])

        merged = jax.nn.sigmoid(ga) * y_a + jax.nn.sigmoid(gn) * y_b
        x = x + jnp.einsum('bsd,de->bse', merged, w_out[l])

        h2 = rms_norm(x, norm2_g[l])
        x = x + expert_choice_ffn(h2, w_router[l], w_gate[l], w_up[l], w_down[l])
    return x
```

```python
import functools
import math

import jax
import jax.numpy as jnp
import numpy as np
from jax import lax
from jax.experimental import pallas as pl
from jax.experimental.pallas import tpu as pltpu

F32 = jnp.float32
BF16 = jnp.bfloat16

HEAD_DIM = 64
DIL_PAIRS = ((128, 1), (512, 4), (2048, 16))
DIL_GROUP_HEADS = 4
DIL_WIDTH = DIL_GROUP_HEADS * len(DIL_PAIRS) * HEAD_DIM
DIL_OUT = DIL_GROUP_HEADS * HEAD_DIM
NA_HEADS = 8
NA_WIDTH = NA_HEADS * HEAD_DIM
GRID_W = 64
NA_KH = 8
NA_KW = 16
N_EXPERTS = 16
EC_CAPACITY_FACTOR = 2
ROPE_THETA = 10000.0
EPS = 1e-6
MASKED = -1e30

LANE_CHUNK = 256
QUAD = 4 * HEAD_DIM
STAT_LANES = 128
VMEM_LIMIT = 56 * 1024 * 1024


def _cparams(sem):
    return pltpu.CompilerParams(dimension_semantics=sem, vmem_limit_bytes=VMEM_LIMIT)


def _const_spec(shape):
    nd = len(shape)
    return pl.BlockSpec(shape, lambda *_: (0,) * nd, pipeline_mode=pl.Buffered(1))


def _proj_kernel(segs, x_ref, g1_ref, w_ref, gain_ref, cos_ref, sin_ref, bd_ref, *out_refs):
    x = x_ref[...]
    ms = jnp.mean(x * x, axis=-1, keepdims=True)
    h = (x * lax.rsqrt(ms + EPS) * g1_ref[...]).astype(BF16)
    bd = bd_ref[...]
    lane = lax.broadcasted_iota(jnp.int32, (1, LANE_CHUNK), 1)
    first_half = (lane % HEAD_DIM) < (HEAD_DIM // 2)

    def head_norm(p, c0):
        ss = jnp.dot((p * p).astype(BF16), bd, preferred_element_type=F32)
        return p * lax.rsqrt(ss * (1.0 / HEAD_DIM) + EPS) * gain_ref[:, c0:c0 + LANE_CHUNK]

    def rope(y):
        partner = jnp.where(first_half,
                            pltpu.roll(y, LANE_CHUNK - HEAD_DIM // 2, 1),
                            pltpu.roll(y, HEAD_DIM // 2, 1))
        return y * cos_ref[...] + partner * sin_ref[...]

    col = 0
    for ref, (width, kind) in zip(out_refs, segs):
        for c in range(0, width, LANE_CHUNK):
            p = jnp.dot(h, w_ref[:, col + c:col + c + LANE_CHUNK], preferred_element_type=F32)
            if kind == "qk_rope":
                p = rope(head_norm(p, col + c))
            elif kind == "qk":
                p = head_norm(p, col + c)
            elif kind == "gate":
                p = 1.0 / (1.0 + jnp.exp(-p))
            ref[:, c:c + LANE_CHUNK] = p.astype(ref.dtype)
        col += width


def _proj_call(x2, g1, w_in, gain_cols, cos_t, sin_t, bd, S, d_model, tm):
    M = x2.shape[0]
    segs = ((DIL_WIDTH, "qk_rope"), (DIL_WIDTH, "qk_rope"), (DIL_WIDTH, "v"),
            (NA_WIDTH, "qk"), (NA_WIDTH, "qk"), (NA_WIDTH, "v"),
            (d_model, "gate"), (d_model, "gate"))
    n_cols = sum(w for w, _ in segs)
    pos_blocks = S // tm
    row = lambda i: (i, 0)
    return pl.pallas_call(
        functools.partial(_proj_kernel, segs),
        grid=(M // tm,),
        in_specs=[
            pl.BlockSpec((tm, d_model), row),
            _const_spec((1, d_model)),
            _const_spec((d_model, n_cols)),
            _const_spec((1, n_cols)),
            pl.BlockSpec((tm, LANE_CHUNK), lambda i: (i % pos_blocks, 0)),
            pl.BlockSpec((tm, LANE_CHUNK), lambda i: (i % pos_blocks, 0)),
            _const_spec((LANE_CHUNK, LANE_CHUNK)),
        ],
        out_specs=[pl.BlockSpec((tm, w), row) for w, _ in segs],
        out_shape=[jax.ShapeDtypeStruct((M, w), BF16) for w, _ in segs],
        compiler_params=_cparams(("parallel",)),
        name="proj",
    )(x2, g1, w_in, gain_cols, cos_t, sin_t, bd)


def _stack_heads(q, lane_head):
    zero = jnp.zeros_like(q)
    return jnp.concatenate([jnp.where(lane_head == h, q, zero) for h in range(4)], axis=0)


def _pick_heads(parts, lane_head):
    out = parts[3]
    for h in (2, 1, 0):
        out = jnp.where(lane_head == h, parts[h], out)
    return out


def _dil_kernel(L, T, R, q_ref, k_ref, v_ref, o_ref, st_ref):
    KW = T + 2 * R
    lane_head = lax.broadcasted_iota(jnp.int32, (1, QUAD), 1) // HEAD_DIM
    stat_lane = lax.broadcasted_iota(jnp.int32, (1, STAT_LANES), 1)
    qi = lax.broadcasted_iota(jnp.int32, (T, KW), 0)
    kj = lax.broadcasted_iota(jnp.int32, (T, KW), 1)

    def body(i, carry):
        q0 = pl.multiple_of(i * T, T)
        ks = pl.multiple_of(jnp.clip(q0 - R, 0, L - KW), R)
        q = q_ref[pl.ds(q0, T), :]
        k = k_ref[pl.ds(ks, KW), :]
        v = v_ref[pl.ds(ks, KW), :]
        off = kj - qi + (ks - q0)
        bias = jnp.where(jnp.abs(off) <= R, 0.0, MASKED).astype(F32)
        s = lax.dot_general(_stack_heads(q, lane_head), k, (((1,), (1,)), ((), ())),
                            preferred_element_type=F32)
        outs = []
        stat = jnp.zeros((T, STAT_LANES), F32)
        for h in range(4):
            sh = s[h * T:(h + 1) * T] + bias
            m = jnp.max(sh, axis=-1, keepdims=True)
            p = jnp.exp(sh - m)
            den = jnp.sum(p, axis=-1, keepdims=True)
            oh = jnp.dot(p.astype(BF16), v, preferred_element_type=F32)
            outs.append(oh / den)
            stat = jnp.where(stat_lane == h, m, stat)
            stat = jnp.where(stat_lane == 4 + h, den, stat)
        o_ref[pl.ds(q0, T), :] = _pick_heads(outs, lane_head).astype(o_ref.dtype)
        st_ref[pl.ds(q0, T), :] = stat
        return carry

    lax.fori_loop(0, L // T, body, 0)


def _dilated_call(qa, ka, va, group, window, dilation, T=128):
    B, S, _ = qa.shape
    d = dilation
    R = (window // 2) // d
    L = S // d
    assert L % T == 0 and L >= T + 2 * R and R % 16 == 0
    view = lambda t: t.reshape(B, L, d * DIL_WIDTH)
    n_quads = DIL_WIDTH // QUAD
    in_spec = pl.BlockSpec((None, L, QUAD), lambda b, r: (b, 0, r * n_quads + group))
    o, st = pl.pallas_call(
        functools.partial(_dil_kernel, L, T, R),
        grid=(B, d),
        in_specs=[in_spec, in_spec, in_spec],
        out_specs=[pl.BlockSpec((None, L, QUAD), lambda b, r: (b, 0, r)),
                   pl.BlockSpec((None, L, STAT_LANES), lambda b, r: (b, 0, r))],
        out_shape=[jax.ShapeDtypeStruct((B, L, d * QUAD), BF16),
                   jax.ShapeDtypeStruct((B, L, d * STAT_LANES), F32)],
        compiler_params=_cparams(("parallel", "parallel")),
        name=f"dilated_g{group}",
    )(view(qa), view(ka), view(va))
    return o.reshape(B * S, QUAD), st.reshape(B * S, STAT_LANES)


def _na_kernel(rows, RB, q_ref, k_ref, v_ref, bias_ref, o_ref):
    lane_head = lax.broadcasted_iota(jnp.int32, (1, QUAD), 1) // HEAD_DIM
    n_keys = NA_KH * GRID_W
    i = pl.program_id(1)
    for rr in range(RB):
        r = i * RB + rr
        rs = jnp.clip(r - NA_KH // 2, 0, rows - NA_KH)
        e = rs - r + NA_KH - 1
        k0 = pl.multiple_of(rs * GRID_W, GRID_W)
        for quad in range(NA_WIDTH // QUAD):
            lanes = slice(quad * QUAD, (quad + 1) * QUAD)
            q = q_ref[rr * GRID_W:(rr + 1) * GRID_W, lanes]
            k = k_ref[pl.ds(k0, n_keys), lanes]
            v = v_ref[pl.ds(k0, n_keys), lanes]
            s = lax.dot_general(_stack_heads(q, lane_head), k, (((1,), (1,)), ((), ())),
                                preferred_element_type=F32)
            outs = []
            for h in range(4):
                sh = s[h * GRID_W:(h + 1) * GRID_W] + bias_ref[quad * 4 + h, e]
                m = jnp.max(sh, axis=-1, keepdims=True)
                p = jnp.exp(sh - m)
                den = jnp.sum(p, axis=-1, keepdims=True)
                oh = jnp.dot(p.astype(BF16), v, preferred_element_type=F32)
                outs.append(oh / den)
            o_ref[rr * GRID_W:(rr + 1) * GRID_W, lanes] = _pick_heads(outs, lane_head).astype(o_ref.dtype)


def _na_bias_table(rpb):
    c = np.arange(GRID_W)
    ws = np.clip(c - NA_KW // 2, 0, GRID_W - NA_KW)
    col_ok = (c[None, :] >= ws[:, None]) & (c[None, :] < ws[:, None] + NA_KW)
    dc = np.clip(c[None, :] - c[:, None] + NA_KW - 1, 0, 2 * NA_KW - 2)
    dr = np.arange(NA_KH)[:, None] + np.arange(NA_KH)[None, :]
    t = rpb[:, dr[:, :, None, None], dc[None, None]]
    t = jnp.where(col_ok[None, None, None], t.astype(F32), MASKED)
    return t.transpose(0, 1, 3, 2, 4).reshape(rpb.shape[0], NA_KH, GRID_W, NA_KH * GRID_W)


def _na_call(qn, kn, vn, bias, RB=8):
    B, S, _ = qn.shape
    rows = S // GRID_W
    assert rows >= NA_KH and rows % RB == 0
    blk = pl.BlockSpec((None, RB * GRID_W, NA_WIDTH), lambda b, i: (b, i, 0))
    full = pl.BlockSpec((None, S, NA_WIDTH), lambda b, i: (b, 0, 0))
    return pl.pallas_call(
        functools.partial(_na_kernel, rows, RB),
        grid=(B, rows // RB),
        in_specs=[blk, full, full, _const_spec(bias.shape)],
        out_specs=blk,
        out_shape=jax.ShapeDtypeStruct((B, S, NA_WIDTH), BF16),
        compiler_params=_cparams(("parallel", "arbitrary")),
        name="neighbourhood",
    )(qn, kn, vn, bias)


def _mix_kernel(x_ref, o0_ref, o1_ref, o2_ref, s0_ref, s1_ref, s2_ref, yb_ref, ga_ref, gn_ref,
                pa_ref, pb_ref, wo_ref, g2_ref, wrh_ref, wrl_ref, x1_ref, h2_ref, aff_ref):
    lane_head = lax.broadcasted_iota(jnp.int32, (1, QUAD), 1) // HEAD_DIM
    stats = [s0_ref[...], s1_ref[...], s2_ref[...]]
    outs = [o0_ref[...].astype(F32), o1_ref[...].astype(F32), o2_ref[...].astype(F32)]
    coef = [None, None, None]
    for h in range(4):
        ms = [st[:, h:h + 1] for st in stats]
        dens = [st[:, 4 + h:5 + h] for st in stats]
        top = jnp.maximum(jnp.maximum(ms[0], ms[1]), ms[2])
        ws = [dens[g] * jnp.exp(ms[g] - top) for g in range(3)]
        tot = ws[0] + ws[1] + ws[2]
        for g in range(3):
            cg = ws[g] / tot
            coef[g] = cg if coef[g] is None else jnp.where(lane_head == h, cg, coef[g])
    ya = coef[0] * outs[0] + coef[1] * outs[1] + coef[2] * outs[2]
    ta = jnp.dot(ya.astype(BF16), pa_ref[...], preferred_element_type=F32)
    tb = jnp.dot(yb_ref[...], pb_ref[...], preferred_element_type=F32)
    merged = ga_ref[...].astype(F32) * ta + gn_ref[...].astype(F32) * tb
    x1 = x_ref[...] + jnp.dot(merged.astype(BF16), wo_ref[...], preferred_element_type=F32)
    x1_ref[...] = x1
    ms2 = jnp.mean(x1 * x1, axis=-1, keepdims=True)
    h2 = x1 * lax.rsqrt(ms2 + EPS) * g2_ref[...]
    h2_ref[...] = h2
    h_hi = h2.astype(BF16)
    h_lo = (h2 - h_hi.astype(F32)).astype(BF16)
    nt = (((1,), (1,)), ((), ()))
    logits = (lax.dot_general(wrh_ref[...], h_hi, nt, preferred_element_type=F32)
              + lax.dot_general(wrh_ref[...], h_lo, nt, preferred_element_type=F32)
              + lax.dot_general(wrl_ref[...], h_hi, nt, preferred_element_type=F32))
    z = jnp.exp(logits - jnp.max(logits, axis=0, keepdims=True))
    aff_ref[...] = z / jnp.sum(z, axis=0, keepdims=True)


def _mix_call(x2, o_parts, st_parts, yb, ga, gn, pa, pb, wo, g2, wr_hi, wr_lo, B, S, tm):
    M, D = x2.shape
    row = lambda i: (i, 0)
    blocks = S // tm
    rows_of = lambda w: pl.BlockSpec((tm, w), row)
    return pl.pallas_call(
        _mix_kernel,
        grid=(M // tm,),
        in_specs=[rows_of(D)] + [rows_of(QUAD)] * 3 + [rows_of(STAT_LANES)] * 3
                 + [rows_of(NA_WIDTH), rows_of(D), rows_of(D),
                    _const_spec(pa.shape), _const_spec(pb.shape), _const_spec(wo.shape),
                    _const_spec(g2.shape), _const_spec(wr_hi.shape), _const_spec(wr_lo.shape)],
        out_specs=[rows_of(D), rows_of(D),
                   pl.BlockSpec((None, N_EXPERTS, tm), lambda i: (i // blocks, 0, i % blocks))],
        out_shape=[jax.ShapeDtypeStruct((M, D), F32), jax.ShapeDtypeStruct((M, D), F32),
                   jax.ShapeDtypeStruct((B, N_EXPERTS, S), F32)],
        compiler_params=_cparams(("parallel",)),
        name="mix",
    )(x2, *o_parts, *st_parts, yb, ga, gn, pa, pb, wo, g2, wr_hi, wr_lo)


def _moe_kernel(cap, idx_ref, gate_ref, x1_hbm, h2_hbm, wg_ref, wu_ref, wd_ref, out_ref,
                xin_ref, y_ref, sem):
    b = pl.program_id(0)
    e = pl.program_id(1)
    base = (b * pl.num_programs(1) + e) * cap

    @pl.when(e == 0)
    def _():
        pltpu.sync_copy(x1_hbm.at[pl.ds(b, 1)], out_ref)

    def row_copy(c):
        t = idx_ref[base + c]
        return pltpu.make_async_copy(h2_hbm.at[b, pl.ds(t, 1), :], xin_ref.at[pl.ds(c, 1), :], sem)

    def issue(c, carry):
        row_copy(c).start()
        return carry

    lax.fori_loop(0, cap, issue, 0)

    def drain(c, carry):
        row_copy(c).wait()
        return carry

    lax.fori_loop(0, cap, drain, 0)

    xb = xin_ref[...].astype(BF16)
    a = jnp.dot(xb, wg_ref[...], preferred_element_type=F32)
    u = jnp.dot(xb, wu_ref[...], preferred_element_type=F32)
    hm = (a * (1.0 / (1.0 + jnp.exp(-a))) * u).astype(BF16)
    y_ref[...] = jnp.dot(hm, wd_ref[...], preferred_element_type=F32)

    def scatter(c, carry):
        t = idx_ref[base + c]
        g = gate_ref[base + c]
        out_ref[0, pl.ds(t, 1), :] = out_ref[0, pl.ds(t, 1), :] + y_ref[pl.ds(c, 1), :] * g
        return carry

    lax.fori_loop(0, cap, scatter, 0)


def _moe_call(idx_flat, gate_flat, x1, h2, wg, wu, wd, cap):
    B, S, D = x1.shape
    E, _, FF = wg.shape
    smem = pl.BlockSpec(memory_space=pltpu.SMEM)
    hbm = pl.BlockSpec(memory_space=pl.ANY)
    return pl.pallas_call(
        functools.partial(_moe_kernel, cap),
        grid=(B, E),
        in_specs=[smem, smem, hbm, hbm,
                  pl.BlockSpec((None, D, FF), lambda b, e: (e, 0, 0)),
                  pl.BlockSpec((None, D, FF), lambda b, e: (e, 0, 0)),
                  pl.BlockSpec((None, FF, D), lambda b, e: (e, 0, 0))],
        out_specs=pl.BlockSpec((1, S, D), lambda b, e: (b, 0, 0)),
        out_shape=jax.ShapeDtypeStruct((B, S, D), F32),
        scratch_shapes=[pltpu.VMEM((cap, D), F32), pltpu.VMEM((cap, D), F32),
                        pltpu.SemaphoreType.DMA(())],
        compiler_params=_cparams(("arbitrary", "arbitrary")),
        name="moe",
    )(idx_flat, gate_flat, x1, h2, wg, wu, wd)


def _rope_tables(S):
    half = HEAD_DIM // 2
    inv = ROPE_THETA ** (-jnp.arange(half, dtype=F32) / half)
    ang = jnp.arange(S, dtype=F32)[:, None] * inv[None, :]
    cos = jnp.cos(ang)
    sin = jnp.sin(ang)
    reps = LANE_CHUNK // HEAD_DIM
    return (jnp.tile(jnp.concatenate([cos, cos], axis=-1), (1, reps)),
            jnp.tile(jnp.concatenate([-sin, sin], axis=-1), (1, reps)))


def _layer(x, norm1_g, w_in, dil_q_g, dil_k_g, na_q_g, na_k_g, na_rpb, w_dil_branch, w_na_branch,
           w_out, norm2_g, w_router, w_gate, w_up, w_down):
    B, S, D = x.shape
    M = B * S
    tm = 512
    scale = HEAD_DIM ** -0.5
    dil_heads = DIL_WIDTH // HEAD_DIM
    gain_cols = jnp.concatenate([
        jnp.tile(dil_q_g * scale, dil_heads), jnp.tile(dil_k_g, dil_heads), jnp.ones((DIL_WIDTH,), F32),
        jnp.tile(na_q_g * scale, NA_HEADS), jnp.tile(na_k_g, NA_HEADS), jnp.ones((NA_WIDTH + 2 * D,), F32),
    ])[None, :].astype(F32)
    cos_t, sin_t = _rope_tables(S)
    blk = np.arange(LANE_CHUNK) // HEAD_DIM
    bd = jnp.asarray(blk[:, None] == blk[None, :], BF16)

    x2 = x.reshape(M, D)
    qa, ka, va, qn, kn, vn, ga, gn = _proj_call(
        x2, norm1_g[None, :], w_in.astype(BF16), gain_cols, cos_t, sin_t, bd, S, D, tm)

    o_parts, st_parts = [], []
    for g, (window, dilation) in enumerate(DIL_PAIRS):
        o, st = _dilated_call(qa.reshape(B, S, -1), ka.reshape(B, S, -1), va.reshape(B, S, -1),
                              g, window, dilation)
        o_parts.append(o)
        st_parts.append(st)

    yb = _na_call(qn.reshape(B, S, -1), kn.reshape(B, S, -1), vn.reshape(B, S, -1),
                  _na_bias_table(na_rpb)).reshape(M, NA_WIDTH)

    wr_t = w_router.T
    wr_hi = wr_t.astype(BF16)
    wr_lo = (wr_t - wr_hi.astype(F32)).astype(BF16)
    x1, h2, aff = _mix_call(x2, o_parts, st_parts, yb, ga, gn,
                            w_dil_branch.astype(BF16), w_na_branch.astype(BF16), w_out.astype(BF16),
                            norm2_g[None, :], wr_hi, wr_lo, B, S, tm)

    cap = (EC_CAPACITY_FACTOR * S) // N_EXPERTS
    gates, idx = lax.top_k(aff, cap)
    return _moe_call(idx.reshape(-1).astype(jnp.int32), gates.reshape(-1),
                     x1.reshape(B, S, D), h2.reshape(B, S, D),
                     w_gate.astype(BF16), w_up.astype(BF16), w_down.astype(BF16), cap)


def kernel(x, norm1_g, w_in, dil_q_norm_g, dil_k_norm_g, na_q_norm_g, na_k_norm_g, na_rpb, w_dil_branch,
           w_na_branch, w_out, norm2_g, w_router, w_gate, w_up, w_down):
    for l in range(norm1_g.shape[0]):
        x = _layer(x, norm1_g[l], w_in[l], dil_q_norm_g[l], dil_k_norm_g[l], na_q_norm_g[l],
                   na_k_norm_g[l], na_rpb[l], w_dil_branch[l], w_na_branch[l], w_out[l], norm2_g[l],
                   w_router[l], w_gate[l], w_up[l], w_down[l])
    return x
```

```python
import functools
import math

import jax
import jax.numpy as jnp
import numpy as np
from jax import lax
from jax.experimental import pallas as pl
from jax.experimental.pallas import tpu as pltpu

F32 = jnp.float32
BF16 = jnp.bfloat16

HEAD_DIM = 64
DIL_PAIRS = ((128, 1), (512, 4), (2048, 16))
DIL_GROUP_HEADS = 4
DIL_WIDTH = DIL_GROUP_HEADS * len(DIL_PAIRS) * HEAD_DIM
DIL_OUT = DIL_GROUP_HEADS * HEAD_DIM
NA_HEADS = 8
NA_WIDTH = NA_HEADS * HEAD_DIM
GRID_W = 64
NA_KH = 8
NA_KW = 16
N_EXPERTS = 16
EC_CAPACITY_FACTOR = 2
ROPE_THETA = 10000.0
EPS = 1e-6
MASKED = -1e30

LANE_CHUNK = 256
QUAD = 4 * HEAD_DIM
STAT_LANES = 128
LANES = 128
SLAB_PAD = 8
SCATTER_BATCH = 8
VMEM_LIMIT = 56 * 1024 * 1024


def _cparams(sem):
    return pltpu.CompilerParams(dimension_semantics=sem, vmem_limit_bytes=VMEM_LIMIT)


def _const_spec(shape):
    nd = len(shape)
    return pl.BlockSpec(shape, lambda *_: (0,) * nd, pipeline_mode=pl.Buffered(1))


def _proj_kernel(segs, x_ref, g1_ref, w_ref, gain_ref, cos_ref, sin_ref, bd_ref, *out_and_scratch):
    *out_refs, tmp_ref = out_and_scratch
    x = x_ref[...]
    ms = jnp.mean(x * x, axis=-1, keepdims=True)
    h = (x * lax.rsqrt(ms + EPS) * g1_ref[...]).astype(BF16)
    bd = bd_ref[...]
    lane = lax.broadcasted_iota(jnp.int32, (1, LANE_CHUNK), 1)
    first_half = (lane % HEAD_DIM) < (HEAD_DIM // 2)

    def head_norm(p, c0):
        ss = jnp.dot((p * p).astype(BF16), bd, preferred_element_type=F32)
        return p * lax.rsqrt(ss * (1.0 / HEAD_DIM) + EPS) * gain_ref[:, c0:c0 + LANE_CHUNK]

    def rope(y):
        partner = jnp.where(first_half,
                            pltpu.roll(y, LANE_CHUNK - HEAD_DIM // 2, 1),
                            pltpu.roll(y, HEAD_DIM // 2, 1))
        return y * cos_ref[...] + partner * sin_ref[...]

    tm = x.shape[0]
    col = 0
    for ref, (width, kind, dil) in zip(out_refs, segs):
        for c in range(0, width, LANE_CHUNK):
            p = jnp.dot(h, w_ref[:, col + c:col + c + LANE_CHUNK], preferred_element_type=F32)
            if kind == "qk_rope":
                p = rope(head_norm(p, col + c))
            elif kind == "qk":
                p = head_norm(p, col + c)
            elif kind == "gate":
                p = 1.0 / (1.0 + jnp.exp(-p))
            if dil == 1:
                ref[:, c:c + LANE_CHUNK] = p.astype(ref.dtype)
            else:
                for half in range(LANE_CHUNK // LANES):
                    tmp_ref[half] = p[:, half * LANES:(half + 1) * LANES]
                for r in range(dil):
                    for half in range(LANE_CHUNK // LANES):
                        c0 = r * LANE_CHUNK + half * LANES
                        ref[:, c0:c0 + LANES] = tmp_ref[half, pl.ds(r, tm // dil, stride=dil), :].astype(ref.dtype)
        col += width


def _proj_call(x2, g1, w_in, gain_cols, cos_t, sin_t, bd, S, d_model, tm):
    M = x2.shape[0]
    dils = [d for _, d in DIL_PAIRS]
    segs = tuple((QUAD, kind, d) for kind in ("qk_rope", "qk_rope", "v") for d in dils) + (
        (NA_WIDTH, "qk", 1), (NA_WIDTH, "qk", 1), (NA_WIDTH, "v", 1),
        (d_model, "gate", 1), (d_model, "gate", 1))
    n_cols = sum(w for w, _, _ in segs)
    pos_blocks = S // tm
    row = lambda i: (i, 0)

    return pl.pallas_call(
        functools.partial(_proj_kernel, segs),
        grid=(M // tm,),
        in_specs=[
            pl.BlockSpec((tm, d_model), row),
            _const_spec((1, d_model)),
            _const_spec((d_model, n_cols)),
            _const_spec((1, n_cols)),
            pl.BlockSpec((tm, LANE_CHUNK), lambda i: (i % pos_blocks, 0)),
            pl.BlockSpec((tm, LANE_CHUNK), lambda i: (i % pos_blocks, 0)),
            _const_spec((LANE_CHUNK, LANE_CHUNK)),
        ],
        out_specs=[pl.BlockSpec((tm // d, w * d), row) for w, _, d in segs],
        out_shape=[jax.ShapeDtypeStruct((M // d, w * d), BF16) for w, _, d in segs],
        scratch_shapes=[pltpu.VMEM((LANE_CHUNK // LANES, tm, LANES), F32)],
        compiler_params=_cparams(("parallel",)),
        name="proj",
    )(x2, g1, w_in, gain_cols, cos_t, sin_t, bd)


def _stack_heads(q, lane_head):
    zero = jnp.zeros_like(q)
    return jnp.concatenate([jnp.where(lane_head == h, q, zero) for h in range(4)], axis=0)


def _pick_heads(parts, lane_head):
    out = parts[3]
    for h in (2, 1, 0):
        out = jnp.where(lane_head == h, parts[h], out)
    return out


def _dil_kernel(L, T, R, q_ref, k_ref, v_ref, o_ref, st_ref):
    KW = T + 2 * R
    lane_head = lax.broadcasted_iota(jnp.int32, (1, QUAD), 1) // HEAD_DIM
    stat_lane = lax.broadcasted_iota(jnp.int32, (1, STAT_LANES), 1)
    qi = lax.broadcasted_iota(jnp.int32, (T, KW), 0)
    kj = lax.broadcasted_iota(jnp.int32, (T, KW), 1)

    def body(i, carry):
        q0 = pl.multiple_of(i * T, T)
        ks = pl.multiple_of(jnp.clip(q0 - R, 0, L - KW), R)
        q = q_ref[pl.ds(q0, T), :]
        k = k_ref[pl.ds(ks, KW), :]
        v = v_ref[pl.ds(ks, KW), :]
        off = kj - qi + (ks - q0)
        bias = jnp.where(jnp.abs(off) <= R, 0.0, MASKED).astype(F32)
        s = lax.dot_general(_stack_heads(q, lane_head), k, (((1,), (1,)), ((), ())),
                            preferred_element_type=F32)
        outs = []
        stat = jnp.zeros((T, STAT_LANES), F32)
        for h in range(4):
            sh = s[h * T:(h + 1) * T] + bias
            m = jnp.max(sh, axis=-1, keepdims=True)
            p = jnp.exp(sh - m)
            den = jnp.sum(p, axis=-1, keepdims=True)
            oh = jnp.dot(p.astype(BF16), v, preferred_element_type=F32)
            outs.append(oh / den)
            stat = jnp.where(stat_lane == h, m, stat)
            stat = jnp.where(stat_lane == 4 + h, den, stat)
        o_ref[pl.ds(q0, T), :] = _pick_heads(outs, lane_head).astype(o_ref.dtype)
        st_ref[pl.ds(q0, T), :] = stat
        return carry

    lax.fori_loop(0, L // T, body, 0)


def _dilated_call(q, k, v, B, S, group, window, dilation, T=128):
    d = dilation
    R = (window // 2) // d
    L = S // d
    assert L % T == 0 and L >= T + 2 * R and R % 16 == 0
    view = lambda t: t.reshape(B, L, d * QUAD)
    in_spec = pl.BlockSpec((None, L, QUAD), lambda b, r: (b, 0, r))
    o, st = pl.pallas_call(
        functools.partial(_dil_kernel, L, T, R),
        grid=(B, d),
        in_specs=[in_spec, in_spec, in_spec],
        out_specs=[pl.BlockSpec((None, L, QUAD), lambda b, r: (b, 0, r)),
                   pl.BlockSpec((None, L, STAT_LANES), lambda b, r: (b, 0, r))],
        out_shape=[jax.ShapeDtypeStruct((B, L, d * QUAD), BF16),
                   jax.ShapeDtypeStruct((B, L, d * STAT_LANES), F32)],
        compiler_params=_cparams(("parallel", "parallel")),
        name=f"dilated_g{group}",
    )(view(q), view(k), view(v))
    return o.reshape(B * L, d * QUAD), st.reshape(B * L, d * STAT_LANES)


def _na_kernel(rows, RB, q_ref, k_ref, v_ref, bias_ref, o_ref):
    lane_head = lax.broadcasted_iota(jnp.int32, (1, QUAD), 1) // HEAD_DIM
    n_keys = NA_KH * GRID_W
    i = pl.program_id(1)
    for rr in range(RB):
        r = i * RB + rr
        rs = jnp.clip(r - NA_KH // 2, 0, rows - NA_KH)
        e = rs - r + NA_KH - 1
        k0 = pl.multiple_of(rs * GRID_W, GRID_W)
        for quad in range(NA_WIDTH // QUAD):
            lanes = slice(quad * QUAD, (quad + 1) * QUAD)
            q = q_ref[rr * GRID_W:(rr + 1) * GRID_W, lanes]
            k = k_ref[pl.ds(k0, n_keys), lanes]
            v = v_ref[pl.ds(k0, n_keys), lanes]
            s = lax.dot_general(_stack_heads(q, lane_head), k, (((1,), (1,)), ((), ())),
                                preferred_element_type=F32)
            outs = []
            for h in range(4):
                sh = s[h * GRID_W:(h + 1) * GRID_W] + bias_ref[quad * 4 + h, e]
                m = jnp.max(sh, axis=-1, keepdims=True)
                p = jnp.exp(sh - m)
                den = jnp.sum(p, axis=-1, keepdims=True)
                oh = jnp.dot(p.astype(BF16), v, preferred_element_type=F32)
                outs.append(oh / den)
            o_ref[rr * GRID_W:(rr + 1) * GRID_W, lanes] = _pick_heads(outs, lane_head).astype(o_ref.dtype)


def _na_bias_table(rpb):
    c = np.arange(GRID_W)
    ws = np.clip(c - NA_KW // 2, 0, GRID_W - NA_KW)
    col_ok = (c[None, :] >= ws[:, None]) & (c[None, :] < ws[:, None] + NA_KW)
    dc = np.clip(c[None, :] - c[:, None] + NA_KW - 1, 0, 2 * NA_KW - 2)
    n_dc = 2 * NA_KW - 1
    n_heads, n_dr = rpb.shape[0], rpb.shape[1]
    onehot = np.zeros((n_dc, GRID_W * GRID_W), np.float32)
    onehot[dc.reshape(-1), np.arange(GRID_W * GRID_W)] = 1.0
    a = jnp.dot(rpb.reshape(n_heads * n_dr, n_dc).astype(F32), onehot, precision=lax.Precision.HIGHEST)
    a = jnp.where(col_ok[None, None], a.reshape(n_heads, n_dr, GRID_W, GRID_W), MASKED)
    t = jnp.stack([a[:, e:e + NA_KH] for e in range(NA_KH)], axis=1)
    return t.transpose(0, 1, 3, 2, 4).reshape(n_heads, NA_KH, GRID_W, NA_KH * GRID_W)


def _na_call(qn, kn, vn, bias, RB=8):
    B, S, _ = qn.shape
    rows = S // GRID_W
    assert rows >= NA_KH and rows % RB == 0
    blk = pl.BlockSpec((None, RB * GRID_W, NA_WIDTH), lambda b, i: (b, i, 0))
    full = pl.BlockSpec((None, S, NA_WIDTH), lambda b, i: (b, 0, 0))
    return pl.pallas_call(
        functools.partial(_na_kernel, rows, RB),
        grid=(B, rows // RB),
        in_specs=[blk, full, full, _const_spec(bias.shape)],
        out_specs=blk,
        out_shape=jax.ShapeDtypeStruct((B, S, NA_WIDTH), BF16),
        compiler_params=_cparams(("parallel", "arbitrary")),
        name="neighbourhood",
    )(qn, kn, vn, bias)


def _token_major(o_ref, s_ref, o_tmp, s_tmp, dil):
    if dil == 1:
        return o_ref[...].astype(F32), s_ref[...]
    rows = o_ref.shape[0]
    halves = QUAD // LANES
    for r in range(dil):
        blk = o_ref[:, r * QUAD:(r + 1) * QUAD].astype(F32)
        for half in range(halves):
            o_tmp[half, pl.ds(r, rows, stride=dil), :] = blk[:, half * LANES:(half + 1) * LANES]
        s_tmp[pl.ds(r, rows, stride=dil), :] = s_ref[:, r * STAT_LANES:(r + 1) * STAT_LANES]
    return jnp.concatenate([o_tmp[half] for half in range(halves)], axis=1), s_tmp[...]


def _mix_kernel(dils, x_ref, o0_ref, o1_ref, o2_ref, s0_ref, s1_ref, s2_ref, yb_ref, ga_ref, gn_ref,
                pa_ref, pb_ref, wo_ref, g2_ref, wrh_ref, wrl_ref, x1_ref, h2_ref, aff_ref, o_tmp, s_tmp):
    lane_head = lax.broadcasted_iota(jnp.int32, (1, QUAD), 1) // HEAD_DIM
    outs, stats = [], []
    for g, (o_ref, s_ref, dil) in enumerate(zip((o0_ref, o1_ref, o2_ref), (s0_ref, s1_ref, s2_ref), dils)):
        o, st = _token_major(o_ref, s_ref, o_tmp.at[g], s_tmp.at[g], dil)
        outs.append(o)
        stats.append(st)
    coef = [None, None, None]
    for h in range(4):
        ms = [st[:, h:h + 1] for st in stats]
        dens = [st[:, 4 + h:5 + h] for st in stats]
        top = jnp.maximum(jnp.maximum(ms[0], ms[1]), ms[2])
        ws = [dens[g] * jnp.exp(ms[g] - top) for g in range(3)]
        tot = ws[0] + ws[1] + ws[2]
        for g in range(3):
            cg = ws[g] / tot
            coef[g] = cg if coef[g] is None else jnp.where(lane_head == h, cg, coef[g])
    ya = coef[0] * outs[0] + coef[1] * outs[1] + coef[2] * outs[2]
    ta = jnp.dot(ya.astype(BF16), pa_ref[...], preferred_element_type=F32)
    tb = jnp.dot(yb_ref[...], pb_ref[...], preferred_element_type=F32)
    merged = ga_ref[...].astype(F32) * ta + gn_ref[...].astype(F32) * tb
    x1 = x_ref[...] + jnp.dot(merged.astype(BF16), wo_ref[...], preferred_element_type=F32)
    x1_ref[...] = x1
    ms2 = jnp.mean(x1 * x1, axis=-1, keepdims=True)
    h2 = x1 * lax.rsqrt(ms2 + EPS) * g2_ref[...]
    h2_ref[...] = h2
    h_hi = h2.astype(BF16)
    h_lo = (h2 - h_hi.astype(F32)).astype(BF16)
    nt = (((1,), (1,)), ((), ()))
    logits = (lax.dot_general(wrh_ref[...], h_hi, nt, preferred_element_type=F32)
              + lax.dot_general(wrh_ref[...], h_lo, nt, preferred_element_type=F32)
              + lax.dot_general(wrl_ref[...], h_hi, nt, preferred_element_type=F32))
    z = jnp.exp(logits - jnp.max(logits, axis=0, keepdims=True))
    aff_ref[...] = z / jnp.sum(z, axis=0, keepdims=True)


def _mix_call(x2, o_parts, st_parts, yb, ga, gn, pa, pb, wo, g2, wr_hi, wr_lo, B, S, tm):
    M, D = x2.shape
    row = lambda i: (i, 0)
    blocks = S // tm
    rows_of = lambda w, d=1: pl.BlockSpec((tm // d, w * d), row)
    dils = tuple(d for _, d in DIL_PAIRS)
    return pl.pallas_call(
        functools.partial(_mix_kernel, dils),
        grid=(M // tm,),
        in_specs=[rows_of(D)] + [rows_of(QUAD, d) for d in dils] + [rows_of(STAT_LANES, d) for d in dils]
                 + [rows_of(NA_WIDTH), rows_of(D), rows_of(D),
                    _const_spec(pa.shape), _const_spec(pb.shape), _const_spec(wo.shape),
                    _const_spec(g2.shape), _const_spec(wr_hi.shape), _const_spec(wr_lo.shape)],
        out_specs=[rows_of(D), rows_of(D),
                   pl.BlockSpec((None, N_EXPERTS, tm), lambda i: (i // blocks, 0, i % blocks))],
        out_shape=[jax.ShapeDtypeStruct((M, D), F32), jax.ShapeDtypeStruct((M, D), F32),
                   jax.ShapeDtypeStruct((B, N_EXPERTS, S), F32)],
        scratch_shapes=[pltpu.VMEM((len(dils), QUAD // LANES, tm, LANES), F32),
                        pltpu.VMEM((len(dils), tm, LANES), F32)],
        compiler_params=_cparams(("parallel",)),
        name="mix",
    )(x2, *o_parts, *st_parts, yb, ga, gn, pa, pb, wo, g2, wr_hi, wr_lo)


def _select_kernel(cap, n_exp, rows, aff_ref, tri_ref, below_ref, idx_ref, gate_ref):
    n_bits = 31
    bits = [pltpu.bitcast(aff_ref[e * rows:(e + 1) * rows, :], jnp.int32) for e in range(n_exp)]

    def count(mask):
        return jnp.sum(jnp.sum(mask.astype(F32), axis=1, keepdims=True), axis=0, keepdims=True)

    def refine(i, thrs):
        bit = lax.shift_left(jnp.int32(1), n_bits - 1 - i)
        out = []
        for e in range(n_exp):
            cand = thrs[e] | bit
            out.append(jnp.where(count(bits[e] >= cand) >= cap, cand, thrs[e]))
        return tuple(out)

    thrs = lax.fori_loop(0, n_bits, refine, tuple(jnp.zeros((1, 1), jnp.int32) for _ in range(n_exp)))

    tri = tri_ref[...]
    below = below_ref[...]

    def running_count(mask):
        in_row = jnp.dot(mask.astype(BF16), tri, preferred_element_type=F32)
        row_tot = jnp.broadcast_to(in_row[:, LANES - 1:LANES], in_row.shape).astype(BF16)
        return in_row + jnp.dot(below, row_tot, preferred_element_type=F32)

    above = jnp.concatenate([bits[e] > thrs[e] for e in range(n_exp)], axis=0)
    equal = jnp.concatenate([bits[e] == thrs[e] for e in range(n_exp)], axis=0)
    room = jnp.concatenate(
        [jnp.broadcast_to(cap - count(bits[e] > thrs[e]), (rows, LANES)) for e in range(n_exp)], axis=0)
    chosen = above | (equal & (running_count(equal) <= room))
    cnt = running_count(chosen)

    lane = lax.broadcasted_iota(jnp.int32, (1, LANES), 1)
    lane_f = lane.astype(F32)
    slot = lax.broadcasted_iota(jnp.int32, (cap, 1), 0).astype(F32)
    diag = lax.broadcasted_iota(jnp.int32, (rows, LANES), 0) == lax.broadcasted_iota(jnp.int32, (rows, LANES), 1)
    pad = jnp.zeros((LANES - rows, LANES), BF16)
    idx_out = jnp.zeros((cap, LANES), jnp.int32)
    gate_out = jnp.zeros((cap, LANES), F32)
    for e in range(n_exp):
        cnt_e = cnt[e * rows:(e + 1) * rows]
        aff_e = aff_ref[e * rows:(e + 1) * rows, :]
        row_end = jnp.sum(jnp.where(diag, cnt_e[:, LANES - 1:LANES], 0.0), axis=0, keepdims=True)
        row_end = jnp.where(lane < rows, row_end, float(2 * cap))
        row_of = jnp.sum((row_end <= slot).astype(F32), axis=1, keepdims=True)
        pick = (lane_f == row_of).astype(BF16)
        cnt_hi = jnp.floor(cnt_e * (1.0 / 256.0))
        a_hi = aff_e.astype(BF16)
        a_r1 = aff_e - a_hi.astype(F32)
        a_mid = a_r1.astype(BF16)
        a_lo = (a_r1 - a_mid.astype(F32)).astype(BF16)
        pieces = [cnt_hi.astype(BF16), (cnt_e - 256.0 * cnt_hi).astype(BF16), a_hi, a_mid, a_lo]
        rhs = jnp.concatenate([jnp.concatenate([p, pad], axis=0) for p in pieces], axis=1)
        got = jnp.dot(pick, rhs, preferred_element_type=F32)
        row_cnt = 256.0 * got[:, 0:LANES] + got[:, LANES:2 * LANES]
        row_aff = got[:, 2 * LANES:3 * LANES] + got[:, 3 * LANES:4 * LANES] + got[:, 4 * LANES:5 * LANES]
        lane_of = jnp.sum((row_cnt <= slot).astype(F32), axis=1, keepdims=True)
        gate = jnp.sum(jnp.where(lane_f == lane_of, row_aff, 0.0), axis=1, keepdims=True)
        token = (row_of * float(LANES) + lane_of).astype(jnp.int32)
        idx_out = jnp.where(lane == e, token, idx_out)
        gate_out = jnp.where(lane == e, gate, gate_out)
    idx_ref[...] = idx_out
    gate_ref[...] = gate_out


def _select_call(aff, cap):
    B, E, S = aff.shape
    rows = S // LANES
    assert S % LANES == 0 and rows <= LANES and rows % 8 == 0 and cap <= 256 * 256
    r = np.arange(E * rows)
    tri = jnp.asarray(np.arange(LANES)[:, None] <= np.arange(LANES)[None, :], BF16)
    below = jnp.asarray((r[:, None] // rows == r[None, :] // rows) & (r[None, :] < r[:, None]), BF16)
    out_spec = pl.BlockSpec((None, cap, LANES), lambda b: (b, 0, 0))
    idx, gate = pl.pallas_call(
        functools.partial(_select_kernel, cap, E, rows),
        grid=(B,),
        in_specs=[pl.BlockSpec((None, E * rows, LANES), lambda b: (b, 0, 0)),
                  _const_spec(tri.shape), _const_spec(below.shape)],
        out_specs=[out_spec, out_spec],
        out_shape=[jax.ShapeDtypeStruct((B, cap, LANES), jnp.int32),
                   jax.ShapeDtypeStruct((B, cap, LANES), F32)],
        compiler_params=_cparams(("parallel",)),
        name="select",
    )(aff.reshape(B, E * rows, LANES), tri, below)
    unpack = lambda t: t[:, :, :E].transpose(0, 2, 1)
    return unpack(idx), unpack(gate)


def _moe_kernel(cap, S, D, idx_ref, gate_ref, x1_hbm, h2_hbm, wg_ref, wu_ref, wd_ref, out_hbm,
                acc_ref, xin_ref, y_ref, gather_sem, slab_sem):
    n_slab = D // LANES
    acc_pitch = S + SLAB_PAD
    y_pitch = cap + SLAB_PAD
    b = pl.program_id(0)
    e = pl.program_id(1)
    n_e = pl.num_programs(1)
    step = b * n_e + e
    n_steps = pl.num_programs(0) * n_e
    slot = step % 2

    def start_gather(step_, slot_):
        seq = step_ // n_e

        def issue(c, carry):
            t = idx_ref[step_ * cap + c]
            pltpu.make_async_copy(h2_hbm.at[seq, pl.ds(t, 1), :], xin_ref.at[slot_, pl.ds(c, 1), :],
                                  gather_sem.at[slot_]).start()
            return carry

        lax.fori_loop(0, cap, issue, 0, unroll=8)

    @pl.when(step == 0)
    def _():
        start_gather(step, slot)

    @pl.when(step + 1 < n_steps)
    def _():
        start_gather(step + 1, 1 - slot)

    def slab_copy(k, load):
        vm = acc_ref.at[pl.ds(k * acc_pitch, S), :]
        if load:
            return pltpu.make_async_copy(x1_hbm.at[b, :, pl.ds(k * LANES, LANES)], vm, slab_sem)
        return pltpu.make_async_copy(vm, out_hbm.at[b, :, pl.ds(k * LANES, LANES)], slab_sem)

    @pl.when(e == 0)
    def _():
        for k in range(n_slab):
            slab_copy(k, True).start()
        for k in range(n_slab):
            slab_copy(k, True).wait()

    pltpu.make_async_copy(h2_hbm.at[b, pl.ds(0, cap), :], xin_ref.at[slot], gather_sem.at[slot]).wait()

    xb = xin_ref[slot].astype(BF16)
    a = jnp.dot(xb, wg_ref[...], preferred_element_type=F32)
    u = jnp.dot(xb, wu_ref[...], preferred_element_type=F32)
    hm = (a * (1.0 / (1.0 + jnp.exp(-a))) * u).astype(BF16)
    y = jnp.dot(hm, wd_ref[...], preferred_element_type=F32)
    for k in range(n_slab):
        y_ref[k * y_pitch:k * y_pitch + cap, :] = y[:, k * LANES:(k + 1) * LANES]

    def scatter(j, carry):
        updates = []
        for i in range(SCATTER_BATCH):
            c = j * SCATTER_BATCH + i
            t = idx_ref[step * cap + c]
            g = gate_ref[step * cap + c]
            w = y_ref[pl.ds(c, n_slab, stride=y_pitch), :] * g
            updates.append((t, acc_ref[pl.ds(t, n_slab, stride=acc_pitch), :] + w))
        for t, v in updates:
            acc_ref[pl.ds(t, n_slab, stride=acc_pitch), :] = v
        return carry

    lax.fori_loop(0, cap // SCATTER_BATCH, scatter, 0)

    @pl.when(e == n_e - 1)
    def _():
        for k in range(n_slab):
            slab_copy(k, False).start()
        for k in range(n_slab):
            slab_copy(k, False).wait()


def _moe_call(idx_flat, gate_flat, x1, h2, wg, wu, wd, cap):
    B, S, D = x1.shape
    E, _, FF = wg.shape
    assert D % LANES == 0 and cap % SCATTER_BATCH == 0
    n_slab = D // LANES
    smem = pl.BlockSpec(memory_space=pltpu.SMEM)
    hbm = pl.BlockSpec(memory_space=pl.ANY)
    return pl.pallas_call(
        functools.partial(_moe_kernel, cap, S, D),
        grid=(B, E),
        in_specs=[smem, smem, hbm, hbm,
                  pl.BlockSpec((None, D, FF), lambda b, e: (e, 0, 0)),
                  pl.BlockSpec((None, D, FF), lambda b, e: (e, 0, 0)),
                  pl.BlockSpec((None, FF, D), lambda b, e: (e, 0, 0))],
        out_specs=hbm,
        out_shape=jax.ShapeDtypeStruct((B, S, D), F32),
        scratch_shapes=[pltpu.VMEM((n_slab * (S + SLAB_PAD), LANES), F32),
                        pltpu.VMEM((2, cap, D), F32),
                        pltpu.VMEM((n_slab * (cap + SLAB_PAD), LANES), F32),
                        pltpu.SemaphoreType.DMA((2,)),
                        pltpu.SemaphoreType.DMA(())],
        compiler_params=_cparams(("arbitrary", "arbitrary")),
        name="moe",
    )(idx_flat, gate_flat, x1, h2, wg, wu, wd)


def _rope_tables(S):
    half = HEAD_DIM // 2
    inv = ROPE_THETA ** (-jnp.arange(half, dtype=F32) / half)
    ang = jnp.arange(S, dtype=F32)[:, None] * inv[None, :]
    cos = jnp.cos(ang)
    sin = jnp.sin(ang)
    reps = LANE_CHUNK // HEAD_DIM
    return (jnp.tile(jnp.concatenate([cos, cos], axis=-1), (1, reps)),
            jnp.tile(jnp.concatenate([-sin, sin], axis=-1), (1, reps)))


def _layer(x, norm1_g, w_in, dil_q_g, dil_k_g, na_q_g, na_k_g, na_rpb, w_dil_branch, w_na_branch,
           w_out, norm2_g, w_router, w_gate, w_up, w_down):
    B, S, D = x.shape
    M = B * S
    tm = 512
    scale = HEAD_DIM ** -0.5
    dil_heads = DIL_WIDTH // HEAD_DIM
    gain_cols = jnp.concatenate([
        jnp.tile(dil_q_g * scale, dil_heads), jnp.tile(dil_k_g, dil_heads), jnp.ones((DIL_WIDTH,), F32),
        jnp.tile(na_q_g * scale, NA_HEADS), jnp.tile(na_k_g, NA_HEADS), jnp.ones((NA_WIDTH + 2 * D,), F32),
    ])[None, :].astype(F32)
    cos_t, sin_t = _rope_tables(S)
    blk = np.arange(LANE_CHUNK) // HEAD_DIM
    bd = jnp.asarray(blk[:, None] == blk[None, :], BF16)

    x2 = x.reshape(M, D)
    proj = _proj_call(x2, norm1_g[None, :], w_in.astype(BF16), gain_cols, cos_t, sin_t, bd, S, D, tm)
    n_groups = len(DIL_PAIRS)
    qa, ka, va = proj[:n_groups], proj[n_groups:2 * n_groups], proj[2 * n_groups:3 * n_groups]
    qn, kn, vn, ga, gn = proj[3 * n_groups:]

    o_parts, st_parts = [], []
    for g, (window, dilation) in enumerate(DIL_PAIRS):
        o, st = _dilated_call(qa[g], ka[g], va[g], B, S, g, window, dilation)
        o_parts.append(o)
        st_parts.append(st)

    yb = _na_call(qn.reshape(B, S, -1), kn.reshape(B, S, -1), vn.reshape(B, S, -1),
                  _na_bias_table(na_rpb)).reshape(M, NA_WIDTH)

    wr_t = w_router.T
    wr_hi = wr_t.astype(BF16)
    wr_lo = (wr_t - wr_hi.astype(F32)).astype(BF16)
    x1, h2, aff = _mix_call(x2, o_parts, st_parts, yb, ga, gn,
                            w_dil_branch.astype(BF16), w_na_branch.astype(BF16), w_out.astype(BF16),
                            norm2_g[None, :], wr_hi, wr_lo, B, S, tm)

    cap = (EC_CAPACITY_FACTOR * S) // N_EXPERTS
    idx, gates = _select_call(aff, cap)
    return _moe_call(idx.reshape(-1), gates.reshape(-1),
                     x1.reshape(B, S, D), h2.reshape(B, S, D),
                     w_gate.astype(BF16), w_up.astype(BF16), w_down.astype(BF16), cap)


def kernel(x, norm1_g, w_in, dil_q_norm_g, dil_k_norm_g, na_q_norm_g, na_k_norm_g, na_rpb, w_dil_branch,
           w_na_branch, w_out, norm2_g, w_router, w_gate, w_up, w_down):
    for l in range(norm1_g.shape[0]):
        x = _layer(x, norm1_g[l], w_in[l], dil_q_norm_g[l], dil_k_norm_g[l], na_q_norm_g[l],
                   na_k_norm_g[l], na_rpb[l], w_dil_branch[l], w_na_branch[l], w_out[l], norm2_g[l],
                   w_router[l], w_gate[l], w_up[l], w_down[l])
    return x
```

```python
import functools
import math

import jax
import jax.numpy as jnp
import numpy as np
from jax import lax
from jax.experimental import pallas as pl
from jax.experimental.pallas import tpu as pltpu

F32 = jnp.float32
BF16 = jnp.bfloat16

HEAD_DIM = 64
DIL_PAIRS = ((128, 1), (512, 4), (2048, 16))
DIL_GROUP_HEADS = 4
DIL_WIDTH = DIL_GROUP_HEADS * len(DIL_PAIRS) * HEAD_DIM
DIL_OUT = DIL_GROUP_HEADS * HEAD_DIM
NA_HEADS = 8
NA_WIDTH = NA_HEADS * HEAD_DIM
GRID_W = 64
NA_KH = 8
NA_KW = 16
N_EXPERTS = 16
EC_CAPACITY_FACTOR = 2
ROPE_THETA = 10000.0
EPS = 1e-6
MASKED = -1e30

LANE_CHUNK = 256
QUAD = 4 * HEAD_DIM
STAT_LANES = 128
LANES = 128
SLAB_PAD = 8
SCATTER_BATCH = 8
VMEM_LIMIT = 56 * 1024 * 1024


def _cparams(sem):
    return pltpu.CompilerParams(dimension_semantics=sem, vmem_limit_bytes=VMEM_LIMIT)


def _const_spec(shape):
    nd = len(shape)
    return pl.BlockSpec(shape, lambda *_: (0,) * nd, pipeline_mode=pl.Buffered(1))


def _proj_kernel(segs, x_ref, g1_ref, w_ref, gain_ref, cos_ref, sin_ref, bd_ref, *out_and_scratch):
    *out_refs, tmp_ref = out_and_scratch
    x = x_ref[...]
    ms = jnp.mean(x * x, axis=-1, keepdims=True)
    h = (x * lax.rsqrt(ms + EPS) * g1_ref[...]).astype(BF16)
    bd = bd_ref[...]
    lane = lax.broadcasted_iota(jnp.int32, (1, LANE_CHUNK), 1)
    first_half = (lane % HEAD_DIM) < (HEAD_DIM // 2)

    def head_norm(p, c0):
        ss = jnp.dot((p * p).astype(BF16), bd, preferred_element_type=F32)
        return p * lax.rsqrt(ss * (1.0 / HEAD_DIM) + EPS) * gain_ref[:, c0:c0 + LANE_CHUNK]

    def rope(y):
        partner = jnp.where(first_half,
                            pltpu.roll(y, LANE_CHUNK - HEAD_DIM // 2, 1),
                            pltpu.roll(y, HEAD_DIM // 2, 1))
        return y * cos_ref[...] + partner * sin_ref[...]

    tm = x.shape[0]
    col = 0
    for ref, (width, kind, dil) in zip(out_refs, segs):
        for c in range(0, width, LANE_CHUNK):
            p = jnp.dot(h, w_ref[:, col + c:col + c + LANE_CHUNK], preferred_element_type=F32)
            if kind == "qk_rope":
                p = rope(head_norm(p, col + c))
            elif kind == "qk":
                p = head_norm(p, col + c)
            elif kind == "gate":
                p = 1.0 / (1.0 + jnp.exp(-p))
            if dil == 1:
                ref[:, c:c + LANE_CHUNK] = p.astype(ref.dtype)
            else:
                for half in range(LANE_CHUNK // LANES):
                    tmp_ref[half] = p[:, half * LANES:(half + 1) * LANES]
                for r in range(dil):
                    for half in range(LANE_CHUNK // LANES):
                        c0 = r * LANE_CHUNK + half * LANES
                        ref[:, c0:c0 + LANES] = tmp_ref[half, pl.ds(r, tm // dil, stride=dil), :].astype(ref.dtype)
        col += width


def _proj_call(x2, g1, w_in, gain_cols, cos_t, sin_t, bd, S, d_model, tm):
    M = x2.shape[0]
    dils = [d for _, d in DIL_PAIRS]
    segs = tuple((QUAD, kind, d) for kind in ("qk_rope", "qk_rope", "v") for d in dils) + (
        (NA_WIDTH, "qk", 1), (NA_WIDTH, "qk", 1), (NA_WIDTH, "v", 1),
        (d_model, "gate", 1), (d_model, "gate", 1))
    n_cols = sum(w for w, _, _ in segs)
    pos_blocks = S // tm
    row = lambda i: (i, 0)

    return pl.pallas_call(
        functools.partial(_proj_kernel, segs),
        grid=(M // tm,),
        in_specs=[
            pl.BlockSpec((tm, d_model), row),
            _const_spec((1, d_model)),
            _const_spec((d_model, n_cols)),
            _const_spec((1, n_cols)),
            pl.BlockSpec((tm, LANE_CHUNK), lambda i: (i % pos_blocks, 0)),
            pl.BlockSpec((tm, LANE_CHUNK), lambda i: (i % pos_blocks, 0)),
            _const_spec((LANE_CHUNK, LANE_CHUNK)),
        ],
        out_specs=[pl.BlockSpec((tm // d, w * d), row) for w, _, d in segs],
        out_shape=[jax.ShapeDtypeStruct((M // d, w * d), BF16) for w, _, d in segs],
        scratch_shapes=[pltpu.VMEM((LANE_CHUNK // LANES, tm, LANES), F32)],
        compiler_params=_cparams(("parallel",)),
        name="proj",
    )(x2, g1, w_in, gain_cols, cos_t, sin_t, bd)


def _stack_heads(q, lane_head):
    zero = jnp.zeros_like(q)
    return jnp.concatenate([jnp.where(lane_head == h, q, zero) for h in range(4)], axis=0)


def _pick_heads(parts, lane_head):
    out = parts[3]
    for h in (2, 1, 0):
        out = jnp.where(lane_head == h, parts[h], out)
    return out


def _dil_kernel(L, T, R, q_ref, k_ref, v_ref, o_ref, st_ref):
    KW = T + 2 * R
    lane_head = lax.broadcasted_iota(jnp.int32, (1, QUAD), 1) // HEAD_DIM
    stat_lane = lax.broadcasted_iota(jnp.int32, (1, STAT_LANES), 1)
    qi = lax.broadcasted_iota(jnp.int32, (T, KW), 0)
    kj = lax.broadcasted_iota(jnp.int32, (T, KW), 1)

    def body(i, carry):
        q0 = pl.multiple_of(i * T, T)
        ks = pl.multiple_of(jnp.clip(q0 - R, 0, L - KW), R)
        q = q_ref[pl.ds(q0, T), :]
        k = k_ref[pl.ds(ks, KW), :]
        v = v_ref[pl.ds(ks, KW), :]
        off = kj - qi + (ks - q0)
        bias = jnp.where(jnp.abs(off) <= R, 0.0, MASKED).astype(F32)
        s = lax.dot_general(_stack_heads(q, lane_head), k, (((1,), (1,)), ((), ())),
                            preferred_element_type=F32)
        s = s + jnp.concatenate([bias] * 4, axis=0)
        m = jnp.max(s, axis=-1, keepdims=True)
        p = jnp.exp(s - m)
        den = jnp.sum(p, axis=-1, keepdims=True)
        o = jnp.dot(p.astype(BF16), v, preferred_element_type=F32) / den
        stat = jnp.zeros((T, STAT_LANES), F32)
        for h in range(4):
            stat = jnp.where(stat_lane == h, m[h * T:(h + 1) * T], stat)
            stat = jnp.where(stat_lane == 4 + h, den[h * T:(h + 1) * T], stat)
        outs = [o[h * T:(h + 1) * T] for h in range(4)]
        o_ref[pl.ds(q0, T), :] = _pick_heads(outs, lane_head).astype(o_ref.dtype)
        st_ref[pl.ds(q0, T), :] = stat
        return carry

    lax.fori_loop(0, L // T, body, 0, unroll=2)


def _dilated_call(q, k, v, B, S, group, window, dilation, T=128):
    d = dilation
    R = (window // 2) // d
    L = S // d
    assert L % T == 0 and L >= T + 2 * R and R % 16 == 0
    view = lambda t: t.reshape(B, L, d * QUAD)
    in_spec = pl.BlockSpec((None, L, QUAD), lambda b, r: (b, 0, r))
    o, st = pl.pallas_call(
        functools.partial(_dil_kernel, L, T, R),
        grid=(B, d),
        in_specs=[in_spec, in_spec, in_spec],
        out_specs=[pl.BlockSpec((None, L, QUAD), lambda b, r: (b, 0, r)),
                   pl.BlockSpec((None, L, STAT_LANES), lambda b, r: (b, 0, r))],
        out_shape=[jax.ShapeDtypeStruct((B, L, d * QUAD), BF16),
                   jax.ShapeDtypeStruct((B, L, d * STAT_LANES), F32)],
        compiler_params=_cparams(("parallel", "parallel")),
        name=f"dilated_g{group}",
    )(view(q), view(k), view(v))
    return o.reshape(B * L, d * QUAD), st.reshape(B * L, d * STAT_LANES)


def _na_kernel(rows, RB, q_ref, k_ref, v_ref, bias_ref, o_ref):
    lane_head = lax.broadcasted_iota(jnp.int32, (1, QUAD), 1) // HEAD_DIM
    n_keys = NA_KH * GRID_W
    i = pl.program_id(1)
    for rr in range(RB):
        r = i * RB + rr
        rs = jnp.clip(r - NA_KH // 2, 0, rows - NA_KH)
        e = rs - r + NA_KH - 1
        k0 = pl.multiple_of(rs * GRID_W, GRID_W)
        for quad in range(NA_WIDTH // QUAD):
            lanes = slice(quad * QUAD, (quad + 1) * QUAD)
            q = q_ref[rr * GRID_W:(rr + 1) * GRID_W, lanes]
            k = k_ref[pl.ds(k0, n_keys), lanes]
            v = v_ref[pl.ds(k0, n_keys), lanes]
            s = lax.dot_general(_stack_heads(q, lane_head), k, (((1,), (1,)), ((), ())),
                                preferred_element_type=F32)
            s = s + bias_ref[quad, e]
            m = jnp.max(s, axis=-1, keepdims=True)
            p = jnp.exp(s - m)
            den = jnp.sum(p, axis=-1, keepdims=True)
            o = jnp.dot(p.astype(BF16), v, preferred_element_type=F32) / den
            outs = [o[h * GRID_W:(h + 1) * GRID_W] for h in range(4)]
            o_ref[rr * GRID_W:(rr + 1) * GRID_W, lanes] = _pick_heads(outs, lane_head).astype(o_ref.dtype)


def _na_bias_table(rpb):
    c = np.arange(GRID_W)
    ws = np.clip(c - NA_KW // 2, 0, GRID_W - NA_KW)
    col_ok = (c[None, :] >= ws[:, None]) & (c[None, :] < ws[:, None] + NA_KW)
    dc = np.clip(c[None, :] - c[:, None] + NA_KW - 1, 0, 2 * NA_KW - 2)
    n_dc = 2 * NA_KW - 1
    n_heads, n_dr = rpb.shape[0], rpb.shape[1]
    onehot = np.zeros((n_dc, GRID_W * GRID_W), np.float32)
    onehot[dc.reshape(-1), np.arange(GRID_W * GRID_W)] = 1.0
    a = jnp.dot(rpb.reshape(n_heads * n_dr, n_dc).astype(F32), onehot, precision=lax.Precision.HIGHEST)
    a = jnp.where(col_ok[None, None], a.reshape(n_heads, n_dr, GRID_W, GRID_W), MASKED)
    t = jnp.stack([a[:, e:e + NA_KH] for e in range(NA_KH)], axis=1)
    t = t.reshape(n_heads // 4, 4, NA_KH, NA_KH, GRID_W, GRID_W).transpose(0, 2, 1, 4, 3, 5)
    return t.reshape(n_heads // 4, NA_KH, 4 * GRID_W, NA_KH * GRID_W)


def _na_call(qn, kn, vn, bias, RB=8):
    B, S, _ = qn.shape
    rows = S // GRID_W
    assert rows >= NA_KH and rows % RB == 0
    blk = pl.BlockSpec((None, RB * GRID_W, NA_WIDTH), lambda b, i: (b, i, 0))
    full = pl.BlockSpec((None, S, NA_WIDTH), lambda b, i: (b, 0, 0))
    return pl.pallas_call(
        functools.partial(_na_kernel, rows, RB),
        grid=(B, rows // RB),
        in_specs=[blk, full, full, _const_spec(bias.shape)],
        out_specs=blk,
        out_shape=jax.ShapeDtypeStruct((B, S, NA_WIDTH), BF16),
        compiler_params=_cparams(("parallel", "arbitrary")),
        name="neighbourhood",
    )(qn, kn, vn, bias)


def _token_major(o_ref, s_ref, o_tmp, s_tmp, dil):
    if dil == 1:
        return o_ref[...].astype(F32), s_ref[...]
    rows = o_ref.shape[0]
    halves = QUAD // LANES
    for r in range(dil):
        blk = o_ref[:, r * QUAD:(r + 1) * QUAD].astype(F32)
        for half in range(halves):
            o_tmp[half, pl.ds(r, rows, stride=dil), :] = blk[:, half * LANES:(half + 1) * LANES]
        s_tmp[pl.ds(r, rows, stride=dil), :] = s_ref[:, r * STAT_LANES:(r + 1) * STAT_LANES]
    return jnp.concatenate([o_tmp[half] for half in range(halves)], axis=1), s_tmp[...]


def _mix_kernel(dils, x_ref, o0_ref, o1_ref, o2_ref, s0_ref, s1_ref, s2_ref, yb_ref, ga_ref, gn_ref,
                pa_ref, pb_ref, wo_ref, g2_ref, wrh_ref, wrl_ref, x1_ref, h2_ref, aff_ref, o_tmp, s_tmp):
    lane_head = lax.broadcasted_iota(jnp.int32, (1, QUAD), 1) // HEAD_DIM
    outs, stats = [], []
    for g, (o_ref, s_ref, dil) in enumerate(zip((o0_ref, o1_ref, o2_ref), (s0_ref, s1_ref, s2_ref), dils)):
        o, st = _token_major(o_ref, s_ref, o_tmp.at[g], s_tmp.at[g], dil)
        outs.append(o)
        stats.append(st)
    coef = [None, None, None]
    for h in range(4):
        ms = [st[:, h:h + 1] for st in stats]
        dens = [st[:, 4 + h:5 + h] for st in stats]
        top = jnp.maximum(jnp.maximum(ms[0], ms[1]), ms[2])
        ws = [dens[g] * jnp.exp(ms[g] - top) for g in range(3)]
        tot = ws[0] + ws[1] + ws[2]
        for g in range(3):
            cg = ws[g] / tot
            coef[g] = cg if coef[g] is None else jnp.where(lane_head == h, cg, coef[g])
    ya = coef[0] * outs[0] + coef[1] * outs[1] + coef[2] * outs[2]
    ta = jnp.dot(ya.astype(BF16), pa_ref[...], preferred_element_type=F32)
    tb = jnp.dot(yb_ref[...], pb_ref[...], preferred_element_type=F32)
    merged = ga_ref[...].astype(F32) * ta + gn_ref[...].astype(F32) * tb
    x1 = x_ref[...] + jnp.dot(merged.astype(BF16), wo_ref[...], preferred_element_type=F32)
    x1_ref[...] = x1
    ms2 = jnp.mean(x1 * x1, axis=-1, keepdims=True)
    h2 = x1 * lax.rsqrt(ms2 + EPS) * g2_ref[...]
    h2_ref[...] = h2
    h_hi = h2.astype(BF16)
    h_lo = (h2 - h_hi.astype(F32)).astype(BF16)
    nt = (((1,), (1,)), ((), ()))
    logits = (lax.dot_general(wrh_ref[...], h_hi, nt, preferred_element_type=F32)
              + lax.dot_general(wrh_ref[...], h_lo, nt, preferred_element_type=F32)
              + lax.dot_general(wrl_ref[...], h_hi, nt, preferred_element_type=F32))
    z = jnp.exp(logits - jnp.max(logits, axis=0, keepdims=True))
    aff_ref[...] = z / jnp.sum(z, axis=0, keepdims=True)


def _mix_call(x2, o_parts, st_parts, yb, ga, gn, pa, pb, wo, g2, wr_hi, wr_lo, B, S, tm):
    M, D = x2.shape
    row = lambda i: (i, 0)
    blocks = S // tm
    rows_of = lambda w, d=1: pl.BlockSpec((tm // d, w * d), row)
    dils = tuple(d for _, d in DIL_PAIRS)
    return pl.pallas_call(
        functools.partial(_mix_kernel, dils),
        grid=(M // tm,),
        in_specs=[rows_of(D)] + [rows_of(QUAD, d) for d in dils] + [rows_of(STAT_LANES, d) for d in dils]
                 + [rows_of(NA_WIDTH), rows_of(D), rows_of(D),
                    _const_spec(pa.shape), _const_spec(pb.shape), _const_spec(wo.shape),
                    _const_spec(g2.shape), _const_spec(wr_hi.shape), _const_spec(wr_lo.shape)],
        out_specs=[rows_of(D), rows_of(D),
                   pl.BlockSpec((None, N_EXPERTS, tm), lambda i: (i // blocks, 0, i % blocks))],
        out_shape=[jax.ShapeDtypeStruct((M, D), F32), jax.ShapeDtypeStruct((M, D), F32),
                   jax.ShapeDtypeStruct((B, N_EXPERTS, S), F32)],
        scratch_shapes=[pltpu.VMEM((len(dils), QUAD // LANES, tm, LANES), F32),
                        pltpu.VMEM((len(dils), tm, LANES), F32)],
        compiler_params=_cparams(("parallel",)),
        name="mix",
    )(x2, *o_parts, *st_parts, yb, ga, gn, pa, pb, wo, g2, wr_hi, wr_lo)


def _select_kernel(cap, n_exp, rows, aff_ref, tri_ref, below_ref, idx_ref, gate_ref):
    n_bits = 31
    affs = [aff_ref[e * rows:(e + 1) * rows, :] for e in range(n_exp)]
    as_float = lambda word: pltpu.bitcast(word, F32)

    def count(mask):
        return jnp.sum(jnp.sum(mask.astype(F32), axis=1, keepdims=True), axis=0, keepdims=True)

    def refine(i, thrs):
        bit = lax.shift_left(jnp.int32(1), n_bits - 1 - i)
        out = []
        for e in range(n_exp):
            cand = thrs[e] | bit
            out.append(jnp.where(count(affs[e] >= as_float(cand)) >= cap, cand, thrs[e]))
        return tuple(out)

    thrs = lax.fori_loop(0, n_bits, refine, tuple(jnp.zeros((1, 1), jnp.int32) for _ in range(n_exp)))

    tri = tri_ref[...]
    below = below_ref[...]

    def running_count(mask):
        in_row = jnp.dot(mask.astype(BF16), tri, preferred_element_type=F32)
        row_tot = jnp.broadcast_to(in_row[:, LANES - 1:LANES], in_row.shape).astype(BF16)
        return in_row + jnp.dot(below, row_tot, preferred_element_type=F32)

    cuts = [as_float(t) for t in thrs]
    above = jnp.concatenate([affs[e] > cuts[e] for e in range(n_exp)], axis=0)
    equal = jnp.concatenate([affs[e] == cuts[e] for e in range(n_exp)], axis=0)
    room = jnp.concatenate(
        [jnp.broadcast_to(cap - count(affs[e] > cuts[e]), (rows, LANES)) for e in range(n_exp)], axis=0)
    chosen = above | (equal & (running_count(equal) <= room))
    cnt = running_count(chosen)

    lane = lax.broadcasted_iota(jnp.int32, (1, LANES), 1)
    lane_f = lane.astype(F32)
    slot = lax.broadcasted_iota(jnp.int32, (cap, 1), 0).astype(F32)
    diag = lax.broadcasted_iota(jnp.int32, (rows, LANES), 0) == lax.broadcasted_iota(jnp.int32, (rows, LANES), 1)
    pad = jnp.zeros((LANES - rows, LANES), BF16)
    idx_out = jnp.zeros((cap, LANES), jnp.int32)
    gate_out = jnp.zeros((cap, LANES), F32)
    for e in range(n_exp):
        cnt_e = cnt[e * rows:(e + 1) * rows]
        aff_e = aff_ref[e * rows:(e + 1) * rows, :]
        row_end = jnp.sum(jnp.where(diag, cnt_e[:, LANES - 1:LANES], 0.0), axis=0, keepdims=True)
        row_end = jnp.where(lane < rows, row_end, float(2 * cap))
        row_of = jnp.sum((row_end <= slot).astype(F32), axis=1, keepdims=True)
        pick = (lane_f == row_of).astype(BF16)
        cnt_hi = jnp.floor(cnt_e * (1.0 / 256.0))
        a_hi = aff_e.astype(BF16)
        a_r1 = aff_e - a_hi.astype(F32)
        a_mid = a_r1.astype(BF16)
        a_lo = (a_r1 - a_mid.astype(F32)).astype(BF16)
        pieces = [cnt_hi.astype(BF16), (cnt_e - 256.0 * cnt_hi).astype(BF16), a_hi, a_mid, a_lo]
        rhs = jnp.concatenate([jnp.concatenate([p, pad], axis=0) for p in pieces], axis=1)
        got = jnp.dot(pick, rhs, preferred_element_type=F32)
        row_cnt = 256.0 * got[:, 0:LANES] + got[:, LANES:2 * LANES]
        row_aff = got[:, 2 * LANES:3 * LANES] + got[:, 3 * LANES:4 * LANES] + got[:, 4 * LANES:5 * LANES]
        lane_of = jnp.sum((row_cnt <= slot).astype(F32), axis=1, keepdims=True)
        gate = jnp.sum(jnp.where(lane_f == lane_of, row_aff, 0.0), axis=1, keepdims=True)
        token = (row_of * float(LANES) + lane_of).astype(jnp.int32)
        idx_out = jnp.where(lane == e, token, idx_out)
        gate_out = jnp.where(lane == e, gate, gate_out)
    idx_ref[...] = idx_out
    gate_ref[...] = gate_out


def _select_call(aff, cap):
    B, E, S = aff.shape
    rows = S // LANES
    assert S % LANES == 0 and rows <= LANES and rows % 8 == 0 and cap <= 256 * 256
    r = np.arange(E * rows)
    tri = jnp.asarray(np.arange(LANES)[:, None] <= np.arange(LANES)[None, :], BF16)
    below = jnp.asarray((r[:, None] // rows == r[None, :] // rows) & (r[None, :] < r[:, None]), BF16)
    out_spec = pl.BlockSpec((None, cap, LANES), lambda b: (b, 0, 0))
    idx, gate = pl.pallas_call(
        functools.partial(_select_kernel, cap, E, rows),
        grid=(B,),
        in_specs=[pl.BlockSpec((None, E * rows, LANES), lambda b: (b, 0, 0)),
                  _const_spec(tri.shape), _const_spec(below.shape)],
        out_specs=[out_spec, out_spec],
        out_shape=[jax.ShapeDtypeStruct((B, cap, LANES), jnp.int32),
                   jax.ShapeDtypeStruct((B, cap, LANES), F32)],
        compiler_params=_cparams(("parallel",)),
        name="select",
    )(aff.reshape(B, E * rows, LANES), tri, below)
    unpack = lambda t: t[:, :, :E].transpose(0, 2, 1)
    return unpack(idx), unpack(gate)


def _moe_kernel(cap, S, D, idx_ref, gate_ref, x1_hbm, h2_hbm, wg_ref, wu_ref, wd_ref, out_hbm,
                acc_ref, h2_ref, xin_ref, y_ref, slab_sem):
    n_slab = D // LANES
    seq_pitch = S + SLAB_PAD
    cap_pitch = cap + SLAB_PAD
    b = pl.program_id(0)
    e = pl.program_id(1)
    n_e = pl.num_programs(1)
    step = b * n_e + e

    def slab_copy(k, hbm, vmem_ref, load):
        vm = vmem_ref.at[pl.ds(k * seq_pitch, S), :]
        hb = hbm.at[b, :, pl.ds(k * LANES, LANES)]
        return pltpu.make_async_copy(hb, vm, slab_sem) if load else pltpu.make_async_copy(vm, hb, slab_sem)

    @pl.when(e == 0)
    def _():
        loads = [slab_copy(k, hbm, ref, True) for hbm, ref in ((x1_hbm, acc_ref), (h2_hbm, h2_ref))
                 for k in range(n_slab)]
        for cp in loads:
            cp.start()
        for cp in loads:
            cp.wait()

    def gather(j, carry):
        for i in range(SCATTER_BATCH):
            c = j * SCATTER_BATCH + i
            t = idx_ref[step * cap + c]
            xin_ref[pl.ds(c, n_slab, stride=cap_pitch), :] = h2_ref[pl.ds(t, n_slab, stride=seq_pitch), :]
        return carry

    lax.fori_loop(0, cap // SCATTER_BATCH, gather, 0)

    xb = jnp.concatenate([xin_ref[k * cap_pitch:k * cap_pitch + cap, :] for k in range(n_slab)],
                         axis=1).astype(BF16)
    a = jnp.dot(xb, wg_ref[...], preferred_element_type=F32)
    u = jnp.dot(xb, wu_ref[...], preferred_element_type=F32)
    hm = (a * (1.0 / (1.0 + jnp.exp(-a))) * u).astype(BF16)
    y = jnp.dot(hm, wd_ref[...], preferred_element_type=F32)
    for k in range(n_slab):
        y_ref[k * cap_pitch:k * cap_pitch + cap, :] = y[:, k * LANES:(k + 1) * LANES]

    def scatter(j, carry):
        updates = []
        for i in range(SCATTER_BATCH):
            c = j * SCATTER_BATCH + i
            t = idx_ref[step * cap + c]
            g = gate_ref[step * cap + c]
            w = y_ref[pl.ds(c, n_slab, stride=cap_pitch), :] * g
            updates.append((t, acc_ref[pl.ds(t, n_slab, stride=seq_pitch), :] + w))
        for t, v in updates:
            acc_ref[pl.ds(t, n_slab, stride=seq_pitch), :] = v
        return carry

    lax.fori_loop(0, cap // SCATTER_BATCH, scatter, 0)

    @pl.when(e == n_e - 1)
    def _():
        stores = [slab_copy(k, out_hbm, acc_ref, False) for k in range(n_slab)]
        for cp in stores:
            cp.start()
        for cp in stores:
            cp.wait()


def _moe_call(idx_flat, gate_flat, x1, h2, wg, wu, wd, cap):
    B, S, D = x1.shape
    E, _, FF = wg.shape
    assert D % LANES == 0 and cap % SCATTER_BATCH == 0
    n_slab = D // LANES
    smem = pl.BlockSpec(memory_space=pltpu.SMEM)
    hbm = pl.BlockSpec(memory_space=pl.ANY)
    return pl.pallas_call(
        functools.partial(_moe_kernel, cap, S, D),
        grid=(B, E),
        in_specs=[smem, smem, hbm, hbm,
                  pl.BlockSpec((None, D, FF), lambda b, e: (e, 0, 0)),
                  pl.BlockSpec((None, D, FF), lambda b, e: (e, 0, 0)),
                  pl.BlockSpec((None, FF, D), lambda b, e: (e, 0, 0))],
        out_specs=hbm,
        out_shape=jax.ShapeDtypeStruct((B, S, D), F32),
        scratch_shapes=[pltpu.VMEM((n_slab * (S + SLAB_PAD), LANES), F32),
                        pltpu.VMEM((n_slab * (S + SLAB_PAD), LANES), F32),
                        pltpu.VMEM((n_slab * (cap + SLAB_PAD), LANES), F32),
                        pltpu.VMEM((n_slab * (cap + SLAB_PAD), LANES), F32),
                        pltpu.SemaphoreType.DMA(())],
        compiler_params=_cparams(("arbitrary", "arbitrary")),
        name="moe",
    )(idx_flat, gate_flat, x1, h2, wg, wu, wd)


def _rope_tables(S):
    half = HEAD_DIM // 2
    inv = ROPE_THETA ** (-jnp.arange(half, dtype=F32) / half)
    ang = jnp.arange(S, dtype=F32)[:, None] * inv[None, :]
    cos = jnp.cos(ang)
    sin = jnp.sin(ang)
    reps = LANE_CHUNK // HEAD_DIM
    return (jnp.tile(jnp.concatenate([cos, cos], axis=-1), (1, reps)),
            jnp.tile(jnp.concatenate([-sin, sin], axis=-1), (1, reps)))


def _layer(x, norm1_g, w_in, dil_q_g, dil_k_g, na_q_g, na_k_g, na_rpb, w_dil_branch, w_na_branch,
           w_out, norm2_g, w_router, w_gate, w_up, w_down):
    B, S, D = x.shape
    M = B * S
    tm = 512
    scale = HEAD_DIM ** -0.5
    dil_heads = DIL_WIDTH // HEAD_DIM
    gain_cols = jnp.concatenate([
        jnp.tile(dil_q_g * scale, dil_heads), jnp.tile(dil_k_g, dil_heads), jnp.ones((DIL_WIDTH,), F32),
        jnp.tile(na_q_g * scale, NA_HEADS), jnp.tile(na_k_g, NA_HEADS), jnp.ones((NA_WIDTH + 2 * D,), F32),
    ])[None, :].astype(F32)
    cos_t, sin_t = _rope_tables(S)
    blk = np.arange(LANE_CHUNK) // HEAD_DIM
    bd = jnp.asarray(blk[:, None] == blk[None, :], BF16)

    x2 = x.reshape(M, D)
    proj = _proj_call(x2, norm1_g[None, :], w_in.astype(BF16), gain_cols, cos_t, sin_t, bd, S, D, tm)
    n_groups = len(DIL_PAIRS)
    qa, ka, va = proj[:n_groups], proj[n_groups:2 * n_groups], proj[2 * n_groups:3 * n_groups]
    qn, kn, vn, ga, gn = proj[3 * n_groups:]

    o_parts, st_parts = [], []
    for g, (window, dilation) in enumerate(DIL_PAIRS):
        o, st = _dilated_call(qa[g], ka[g], va[g], B, S, g, window, dilation)
        o_parts.append(o)
        st_parts.append(st)

    yb = _na_call(qn.reshape(B, S, -1), kn.reshape(B, S, -1), vn.reshape(B, S, -1),
                  _na_bias_table(na_rpb)).reshape(M, NA_WIDTH)

    wr_t = w_router.T
    wr_hi = wr_t.astype(BF16)
    wr_lo = (wr_t - wr_hi.astype(F32)).astype(BF16)
    x1, h2, aff = _mix_call(x2, o_parts, st_parts, yb, ga, gn,
                            w_dil_branch.astype(BF16), w_na_branch.astype(BF16), w_out.astype(BF16),
                            norm2_g[None, :], wr_hi, wr_lo, B, S, tm)

    cap = (EC_CAPACITY_FACTOR * S) // N_EXPERTS
    idx, gates = _select_call(aff, cap)
    return _moe_call(idx.reshape(-1), gates.reshape(-1),
                     x1.reshape(B, S, D), h2.reshape(B, S, D),
                     w_gate.astype(BF16), w_up.astype(BF16), w_down.astype(BF16), cap)


def kernel(x, norm1_g, w_in, dil_q_norm_g, dil_k_norm_g, na_q_norm_g, na_k_norm_g, na_rpb, w_dil_branch,
           w_na_branch, w_out, norm2_g, w_router, w_gate, w_up, w_down):
    for l in range(norm1_g.shape[0]):
        x = _layer(x, norm1_g[l], w_in[l], dil_q_norm_g[l], dil_k_norm_g[l], na_q_norm_g[l],
                   na_k_norm_g[l], na_rpb[l], w_dil_branch[l], w_na_branch[l], w_out[l], norm2_g[l],
                   w_router[l], w_gate[l], w_up[l], w_down[l])
    return x
```

```python
import functools
import math

import jax
import jax.numpy as jnp
import numpy as np
from jax import lax
from jax.experimental import pallas as pl
from jax.experimental.pallas import tpu as pltpu

F32 = jnp.float32
BF16 = jnp.bfloat16

HEAD_DIM = 64
DIL_PAIRS = ((128, 1), (512, 4), (2048, 16))
DIL_GROUP_HEADS = 4
DIL_WIDTH = DIL_GROUP_HEADS * len(DIL_PAIRS) * HEAD_DIM
DIL_OUT = DIL_GROUP_HEADS * HEAD_DIM
NA_HEADS = 8
NA_WIDTH = NA_HEADS * HEAD_DIM
GRID_W = 64
NA_KH = 8
NA_KW = 16
N_EXPERTS = 16
EC_CAPACITY_FACTOR = 2
ROPE_THETA = 10000.0
EPS = 1e-6
MASKED = -1e30

LANE_CHUNK = 256
QUAD = 4 * HEAD_DIM
STAT_LANES = 128
LANES = 128
SLAB_PAD = 8
SCATTER_BATCH = 8
MIX_SUB_ROWS = 512
VMEM_LIMIT = 56 * 1024 * 1024


def _cparams(sem):
    return pltpu.CompilerParams(dimension_semantics=sem, vmem_limit_bytes=VMEM_LIMIT)


def _const_spec(shape):
    nd = len(shape)
    return pl.BlockSpec(shape, lambda *_: (0,) * nd, pipeline_mode=pl.Buffered(1))


def _proj_kernel(segs, x_ref, g1_ref, w_ref, gain_ref, cos_ref, sin_ref, bd_ref, *out_and_scratch):
    *out_refs, tmp_ref = out_and_scratch
    x = x_ref[...]
    ms = jnp.mean(x * x, axis=-1, keepdims=True)
    h = (x * lax.rsqrt(ms + EPS) * g1_ref[...]).astype(BF16)
    bd = bd_ref[...]
    lane = lax.broadcasted_iota(jnp.int32, (1, LANE_CHUNK), 1)
    first_half = (lane % HEAD_DIM) < (HEAD_DIM // 2)

    def head_norm(p, c0):
        ss = jnp.dot((p * p).astype(BF16), bd, preferred_element_type=F32)
        return p * lax.rsqrt(ss * (1.0 / HEAD_DIM) + EPS) * gain_ref[:, c0:c0 + LANE_CHUNK]

    def rope(y):
        partner = jnp.where(first_half,
                            pltpu.roll(y, LANE_CHUNK - HEAD_DIM // 2, 1),
                            pltpu.roll(y, HEAD_DIM // 2, 1))
        return y * cos_ref[...] + partner * sin_ref[...]

    tm = x.shape[0]
    col = 0
    for ref, (width, kind, dil) in zip(out_refs, segs):
        for c in range(0, width, LANE_CHUNK):
            p = jnp.dot(h, w_ref[:, col + c:col + c + LANE_CHUNK], preferred_element_type=F32)
            if kind == "qk_rope":
                p = rope(head_norm(p, col + c))
            elif kind == "qk":
                p = head_norm(p, col + c)
            elif kind == "gate":
                p = 1.0 / (1.0 + jnp.exp(-p))
            if dil == 1:
                ref[:, c:c + LANE_CHUNK] = p.astype(ref.dtype)
            else:
                for half in range(LANE_CHUNK // LANES):
                    tmp_ref[half] = p[:, half * LANES:(half + 1) * LANES]
                for r in range(dil):
                    for half in range(LANE_CHUNK // LANES):
                        c0 = r * LANE_CHUNK + half * LANES
                        ref[:, c0:c0 + LANES] = tmp_ref[half, pl.ds(r, tm // dil, stride=dil), :].astype(ref.dtype)
        col += width


def _proj_call(x2, g1, w_in, gain_cols, cos_t, sin_t, bd, S, d_model, tm):
    M = x2.shape[0]
    dils = [d for _, d in DIL_PAIRS]
    segs = tuple((QUAD, kind, d) for kind in ("qk_rope", "qk_rope", "v") for d in dils) + (
        (NA_WIDTH, "qk", 1), (NA_WIDTH, "qk", 1), (NA_WIDTH, "v", 1),
        (d_model, "gate", 1), (d_model, "gate", 1))
    n_cols = sum(w for w, _, _ in segs)
    pos_blocks = S // tm
    row = lambda i: (i, 0)

    return pl.pallas_call(
        functools.partial(_proj_kernel, segs),
        grid=(M // tm,),
        in_specs=[
            pl.BlockSpec((tm, d_model), row),
            _const_spec((1, d_model)),
            _const_spec((d_model, n_cols)),
            _const_spec((1, n_cols)),
            pl.BlockSpec((tm, LANE_CHUNK), lambda i: (i % pos_blocks, 0)),
            pl.BlockSpec((tm, LANE_CHUNK), lambda i: (i % pos_blocks, 0)),
            _const_spec((LANE_CHUNK, LANE_CHUNK)),
        ],
        out_specs=[pl.BlockSpec((tm // d, w * d), row) for w, _, d in segs],
        out_shape=[jax.ShapeDtypeStruct((M // d, w * d), BF16) for w, _, d in segs],
        scratch_shapes=[pltpu.VMEM((LANE_CHUNK // LANES, tm, LANES), F32)],
        compiler_params=_cparams(("parallel",)),
        name="proj",
    )(x2, g1, w_in, gain_cols, cos_t, sin_t, bd)


def _stack_heads(q, lane_head):
    zero = jnp.zeros_like(q)
    return jnp.concatenate([jnp.where(lane_head == h, q, zero) for h in range(4)], axis=0)


def _pick_heads(parts, lane_head):
    out = parts[3]
    for h in (2, 1, 0):
        out = jnp.where(lane_head == h, parts[h], out)
    return out


def _dil_kernel(L, T, R, q_ref, k_ref, v_ref, o_ref, st_ref):
    KW = T + 2 * R
    lane_head = lax.broadcasted_iota(jnp.int32, (1, QUAD), 1) // HEAD_DIM
    stat_lane = lax.broadcasted_iota(jnp.int32, (1, STAT_LANES), 1)
    qi = lax.broadcasted_iota(jnp.int32, (T, KW), 0)
    kj = lax.broadcasted_iota(jnp.int32, (T, KW), 1)

    def body(i, carry):
        q0 = pl.multiple_of(i * T, T)
        ks = pl.multiple_of(jnp.clip(q0 - R, 0, L - KW), R)
        q = q_ref[pl.ds(q0, T), :]
        k = k_ref[pl.ds(ks, KW), :]
        v = v_ref[pl.ds(ks, KW), :]
        off = kj - qi + (ks - q0)
        bias = jnp.where(jnp.abs(off) <= R, 0.0, MASKED).astype(F32)
        s = lax.dot_general(_stack_heads(q, lane_head), k, (((1,), (1,)), ((), ())),
                            preferred_element_type=F32)
        s = s + jnp.concatenate([bias] * 4, axis=0)
        m = jnp.max(s, axis=-1, keepdims=True)
        p = jnp.exp(s - m)
        den = jnp.sum(p, axis=-1, keepdims=True)
        o = jnp.dot(p.astype(BF16), v, preferred_element_type=F32) / den
        stat = jnp.zeros((T, STAT_LANES), F32)
        for h in range(4):
            stat = jnp.where(stat_lane == h, m[h * T:(h + 1) * T], stat)
            stat = jnp.where(stat_lane == 4 + h, den[h * T:(h + 1) * T], stat)
        outs = [o[h * T:(h + 1) * T] for h in range(4)]
        o_ref[pl.ds(q0, T), :] = _pick_heads(outs, lane_head).astype(o_ref.dtype)
        st_ref[pl.ds(q0, T), :] = stat
        return carry

    lax.fori_loop(0, L // T, body, 0, unroll=2)


def _dilated_call(q, k, v, B, S, group, window, dilation, T=128):
    d = dilation
    R = (window // 2) // d
    L = S // d
    assert L % T == 0 and L >= T + 2 * R and R % 16 == 0
    view = lambda t: t.reshape(B, L, d * QUAD)
    in_spec = pl.BlockSpec((None, L, QUAD), lambda b, r: (b, 0, r))
    o, st = pl.pallas_call(
        functools.partial(_dil_kernel, L, T, R),
        grid=(B, d),
        in_specs=[in_spec, in_spec, in_spec],
        out_specs=[pl.BlockSpec((None, L, QUAD), lambda b, r: (b, 0, r)),
                   pl.BlockSpec((None, L, STAT_LANES), lambda b, r: (b, 0, r))],
        out_shape=[jax.ShapeDtypeStruct((B, L, d * QUAD), BF16),
                   jax.ShapeDtypeStruct((B, L, d * STAT_LANES), F32)],
        compiler_params=_cparams(("parallel", "parallel")),
        name=f"dilated_g{group}",
    )(view(q), view(k), view(v))
    return o.reshape(B * L, d * QUAD), st.reshape(B * L, d * STAT_LANES)


def _na_kernel(rows, RB, q_ref, k_ref, v_ref, bias_ref, o_ref):
    lane_head = lax.broadcasted_iota(jnp.int32, (1, QUAD), 1) // HEAD_DIM
    n_keys = NA_KH * GRID_W
    i = pl.program_id(1)
    for rr in range(RB):
        r = i * RB + rr
        rs = jnp.clip(r - NA_KH // 2, 0, rows - NA_KH)
        e = rs - r + NA_KH - 1
        k0 = pl.multiple_of(rs * GRID_W, GRID_W)
        for quad in range(NA_WIDTH // QUAD):
            lanes = slice(quad * QUAD, (quad + 1) * QUAD)
            q = q_ref[rr * GRID_W:(rr + 1) * GRID_W, lanes]
            k = k_ref[pl.ds(k0, n_keys), lanes]
            v = v_ref[pl.ds(k0, n_keys), lanes]
            s = lax.dot_general(_stack_heads(q, lane_head), k, (((1,), (1,)), ((), ())),
                                preferred_element_type=F32)
            s = s + bias_ref[quad, e]
            m = jnp.max(s, axis=-1, keepdims=True)
            p = jnp.exp(s - m)
            den = jnp.sum(p, axis=-1, keepdims=True)
            o = jnp.dot(p.astype(BF16), v, preferred_element_type=F32) / den
            outs = [o[h * GRID_W:(h + 1) * GRID_W] for h in range(4)]
            o_ref[rr * GRID_W:(rr + 1) * GRID_W, lanes] = _pick_heads(outs, lane_head).astype(o_ref.dtype)


def _na_bias_table(rpb):
    c = np.arange(GRID_W)
    ws = np.clip(c - NA_KW // 2, 0, GRID_W - NA_KW)
    col_ok = (c[None, :] >= ws[:, None]) & (c[None, :] < ws[:, None] + NA_KW)
    dc = np.clip(c[None, :] - c[:, None] + NA_KW - 1, 0, 2 * NA_KW - 2)
    n_dc = 2 * NA_KW - 1
    n_heads, n_dr = rpb.shape[0], rpb.shape[1]
    onehot = np.zeros((n_dc, GRID_W * GRID_W), np.float32)
    onehot[dc.reshape(-1), np.arange(GRID_W * GRID_W)] = 1.0
    a = jnp.dot(rpb.reshape(n_heads * n_dr, n_dc).astype(F32), onehot, precision=lax.Precision.HIGHEST)
    a = jnp.where(col_ok[None, None], a.reshape(n_heads, n_dr, GRID_W, GRID_W), MASKED)
    t = jnp.stack([a[:, e:e + NA_KH] for e in range(NA_KH)], axis=1)
    t = t.reshape(n_heads // 4, 4, NA_KH, NA_KH, GRID_W, GRID_W).transpose(0, 2, 1, 4, 3, 5)
    return t.reshape(n_heads // 4, NA_KH, 4 * GRID_W, NA_KH * GRID_W)


def _na_call(qn, kn, vn, bias, RB=8):
    B, S, _ = qn.shape
    rows = S // GRID_W
    assert rows >= NA_KH and rows % RB == 0
    blk = pl.BlockSpec((None, RB * GRID_W, NA_WIDTH), lambda b, i: (b, i, 0))
    full = pl.BlockSpec((None, S, NA_WIDTH), lambda b, i: (b, 0, 0))
    return pl.pallas_call(
        functools.partial(_na_kernel, rows, RB),
        grid=(B, rows // RB),
        in_specs=[blk, full, full, _const_spec(bias.shape)],
        out_specs=blk,
        out_shape=jax.ShapeDtypeStruct((B, S, NA_WIDTH), BF16),
        compiler_params=_cparams(("parallel", "arbitrary")),
        name="neighbourhood",
    )(qn, kn, vn, bias)


def _token_major(o_ref, s_ref, o_tmp, s_tmp, dil):
    if dil == 1:
        return o_ref[...].astype(F32), s_ref[...]
    rows = o_ref.shape[0]
    halves = QUAD // LANES
    for r in range(dil):
        blk = o_ref[:, r * QUAD:(r + 1) * QUAD].astype(F32)
        for half in range(halves):
            o_tmp[half, pl.ds(r, rows, stride=dil), :] = blk[:, half * LANES:(half + 1) * LANES]
        s_tmp[pl.ds(r, rows, stride=dil), :] = s_ref[:, r * STAT_LANES:(r + 1) * STAT_LANES]
    return jnp.concatenate([o_tmp[half] for half in range(halves)], axis=1), s_tmp[...]


def _mix_kernel(dils, x_ref, o0_ref, o1_ref, o2_ref, s0_ref, s1_ref, s2_ref, yb_ref, ga_ref, gn_ref,
                pa_ref, pb_ref, wo_ref, g2_ref, wrh_ref, wrl_ref, x1_ref, h2_ref, aff_ref, o_tmp, s_tmp):
    tile_outs, tile_stats = [], []
    for g, (o_ref, s_ref, dil) in enumerate(zip((o0_ref, o1_ref, o2_ref), (s0_ref, s1_ref, s2_ref), dils)):
        o, st = _token_major(o_ref, s_ref, o_tmp.at[g], s_tmp.at[g], dil)
        tile_outs.append(o)
        tile_stats.append(st)
    for r0 in range(0, x_ref.shape[0], MIX_SUB_ROWS):
        rows = slice(r0, r0 + MIX_SUB_ROWS)
        _mix_rows([o[rows] for o in tile_outs], [st[rows] for st in tile_stats],
                  x_ref.at[rows], yb_ref.at[rows], ga_ref.at[rows], gn_ref.at[rows],
                  pa_ref, pb_ref, wo_ref, g2_ref, wrh_ref, wrl_ref,
                  x1_ref.at[rows], h2_ref.at[rows], aff_ref.at[:, rows])


def _mix_rows(outs, stats, x_ref, yb_ref, ga_ref, gn_ref, pa_ref, pb_ref, wo_ref, g2_ref, wrh_ref, wrl_ref,
              x1_ref, h2_ref, aff_ref):
    lane_head = lax.broadcasted_iota(jnp.int32, (1, QUAD), 1) // HEAD_DIM
    coef = [None, None, None]
    for h in range(4):
        ms = [st[:, h:h + 1] for st in stats]
        dens = [st[:, 4 + h:5 + h] for st in stats]
        top = jnp.maximum(jnp.maximum(ms[0], ms[1]), ms[2])
        ws = [dens[g] * jnp.exp(ms[g] - top) for g in range(3)]
        tot = ws[0] + ws[1] + ws[2]
        for g in range(3):
            cg = ws[g] / tot
            coef[g] = cg if coef[g] is None else jnp.where(lane_head == h, cg, coef[g])
    ya = coef[0] * outs[0] + coef[1] * outs[1] + coef[2] * outs[2]
    ta = jnp.dot(ya.astype(BF16), pa_ref[...], preferred_element_type=F32)
    tb = jnp.dot(yb_ref[...], pb_ref[...], preferred_element_type=F32)
    merged = ga_ref[...].astype(F32) * ta + gn_ref[...].astype(F32) * tb
    x1 = x_ref[...] + jnp.dot(merged.astype(BF16), wo_ref[...], preferred_element_type=F32)
    x1_ref[...] = x1
    ms2 = jnp.mean(x1 * x1, axis=-1, keepdims=True)
    h2 = x1 * lax.rsqrt(ms2 + EPS) * g2_ref[...]
    h2_ref[...] = h2
    h_hi = h2.astype(BF16)
    h_lo = (h2 - h_hi.astype(F32)).astype(BF16)
    nt = (((1,), (1,)), ((), ()))
    logits = (lax.dot_general(wrh_ref[...], h_hi, nt, preferred_element_type=F32)
              + lax.dot_general(wrh_ref[...], h_lo, nt, preferred_element_type=F32)
              + lax.dot_general(wrl_ref[...], h_hi, nt, preferred_element_type=F32))
    z = jnp.exp(logits - jnp.max(logits, axis=0, keepdims=True))
    aff_ref[...] = z / jnp.sum(z, axis=0, keepdims=True)


def _mix_call(x2, o_parts, st_parts, yb, ga, gn, pa, pb, wo, g2, wr_hi, wr_lo, B, S, tm):
    M, D = x2.shape
    row = lambda i: (i, 0)
    blocks = S // tm
    rows_of = lambda w, d=1: pl.BlockSpec((tm // d, w * d), row)
    dils = tuple(d for _, d in DIL_PAIRS)
    return pl.pallas_call(
        functools.partial(_mix_kernel, dils),
        grid=(M // tm,),
        in_specs=[rows_of(D)] + [rows_of(QUAD, d) for d in dils] + [rows_of(STAT_LANES, d) for d in dils]
                 + [rows_of(NA_WIDTH), rows_of(D), rows_of(D),
                    _const_spec(pa.shape), _const_spec(pb.shape), _const_spec(wo.shape),
                    _const_spec(g2.shape), _const_spec(wr_hi.shape), _const_spec(wr_lo.shape)],
        out_specs=[rows_of(D), rows_of(D),
                   pl.BlockSpec((None, N_EXPERTS, tm), lambda i: (i // blocks, 0, i % blocks))],
        out_shape=[jax.ShapeDtypeStruct((M, D), F32), jax.ShapeDtypeStruct((M, D), F32),
                   jax.ShapeDtypeStruct((B, N_EXPERTS, S), F32)],
        scratch_shapes=[pltpu.VMEM((len(dils), QUAD // LANES, tm, LANES), F32),
                        pltpu.VMEM((len(dils), tm, LANES), F32)],
        compiler_params=_cparams(("parallel",)),
        name="mix",
    )(x2, *o_parts, *st_parts, yb, ga, gn, pa, pb, wo, g2, wr_hi, wr_lo)


def _select_kernel(cap, n_exp, rows, aff_ref, tri_ref, below_ref, idx_ref, gate_ref):
    n_bits = 31
    affs = [aff_ref[e * rows:(e + 1) * rows, :] for e in range(n_exp)]
    as_float = lambda word: pltpu.bitcast(word, F32)

    def count(mask):
        return jnp.sum(jnp.sum(mask.astype(F32), axis=1, keepdims=True), axis=0, keepdims=True)

    def refine(i, thrs):
        bit = lax.shift_left(jnp.int32(1), n_bits - 1 - i)
        out = []
        for e in range(n_exp):
            cand = thrs[e] | bit
            out.append(jnp.where(count(affs[e] >= as_float(cand)) >= cap, cand, thrs[e]))
        return tuple(out)

    thrs = lax.fori_loop(0, n_bits, refine, tuple(jnp.zeros((1, 1), jnp.int32) for _ in range(n_exp)))

    tri = tri_ref[...]
    below = below_ref[...]

    def running_count(mask):
        in_row = jnp.dot(mask.astype(BF16), tri, preferred_element_type=F32)
        row_tot = jnp.broadcast_to(in_row[:, LANES - 1:LANES], in_row.shape).astype(BF16)
        return in_row + jnp.dot(below, row_tot, preferred_element_type=F32)

    cuts = [as_float(t) for t in thrs]
    above = jnp.concatenate([affs[e] > cuts[e] for e in range(n_exp)], axis=0)
    equal = jnp.concatenate([affs[e] == cuts[e] for e in range(n_exp)], axis=0)
    room = jnp.concatenate(
        [jnp.broadcast_to(cap - count(affs[e] > cuts[e]), (rows, LANES)) for e in range(n_exp)], axis=0)
    chosen = above | (equal & (running_count(equal) <= room))
    cnt = running_count(chosen)

    lane = lax.broadcasted_iota(jnp.int32, (1, LANES), 1)
    lane_f = lane.astype(F32)
    slot = lax.broadcasted_iota(jnp.int32, (cap, 1), 0).astype(F32)
    diag = lax.broadcasted_iota(jnp.int32, (rows, LANES), 0) == lax.broadcasted_iota(jnp.int32, (rows, LANES), 1)
    pad = jnp.zeros((LANES - rows, LANES), BF16)
    idx_out = jnp.zeros((cap, LANES), jnp.int32)
    gate_out = jnp.zeros((cap, LANES), F32)
    for e in range(n_exp):
        cnt_e = cnt[e * rows:(e + 1) * rows]
        aff_e = aff_ref[e * rows:(e + 1) * rows, :]
        row_end = jnp.sum(jnp.where(diag, cnt_e[:, LANES - 1:LANES], 0.0), axis=0, keepdims=True)
        row_end = jnp.where(lane < rows, row_end, float(2 * cap))
        row_of = jnp.sum((row_end <= slot).astype(F32), axis=1, keepdims=True)
        pick = (lane_f == row_of).astype(BF16)
        cnt_hi = jnp.floor(cnt_e * (1.0 / 256.0))
        a_hi = aff_e.astype(BF16)
        a_r1 = aff_e - a_hi.astype(F32)
        a_mid = a_r1.astype(BF16)
        a_lo = (a_r1 - a_mid.astype(F32)).astype(BF16)
        pieces = [cnt_hi.astype(BF16), (cnt_e - 256.0 * cnt_hi).astype(BF16), a_hi, a_mid, a_lo]
        rhs = jnp.concatenate([jnp.concatenate([p, pad], axis=0) for p in pieces], axis=1)
        got = jnp.dot(pick, rhs, preferred_element_type=F32)
        row_cnt = 256.0 * got[:, 0:LANES] + got[:, LANES:2 * LANES]
        row_aff = got[:, 2 * LANES:3 * LANES] + got[:, 3 * LANES:4 * LANES] + got[:, 4 * LANES:5 * LANES]
        lane_of = jnp.sum((row_cnt <= slot).astype(F32), axis=1, keepdims=True)
        gate = jnp.sum(jnp.where(lane_f == lane_of, row_aff, 0.0), axis=1, keepdims=True)
        token = (row_of * float(LANES) + lane_of).astype(jnp.int32)
        idx_out = jnp.where(lane == e, token, idx_out)
        gate_out = jnp.where(lane == e, gate, gate_out)
    idx_ref[...] = idx_out
    gate_ref[...] = gate_out


def _select_call(aff, cap):
    B, E, S = aff.shape
    rows = S // LANES
    assert S % LANES == 0 and rows <= LANES and rows % 8 == 0 and cap <= 256 * 256
    r = np.arange(E * rows)
    tri = jnp.asarray(np.arange(LANES)[:, None] <= np.arange(LANES)[None, :], BF16)
    below = jnp.asarray((r[:, None] // rows == r[None, :] // rows) & (r[None, :] < r[:, None]), BF16)
    out_spec = pl.BlockSpec((None, cap, LANES), lambda b: (b, 0, 0))
    idx, gate = pl.pallas_call(
        functools.partial(_select_kernel, cap, E, rows),
        grid=(B,),
        in_specs=[pl.BlockSpec((None, E * rows, LANES), lambda b: (b, 0, 0)),
                  _const_spec(tri.shape), _const_spec(below.shape)],
        out_specs=[out_spec, out_spec],
        out_shape=[jax.ShapeDtypeStruct((B, cap, LANES), jnp.int32),
                   jax.ShapeDtypeStruct((B, cap, LANES), F32)],
        compiler_params=_cparams(("parallel",)),
        name="select",
    )(aff.reshape(B, E * rows, LANES), tri, below)
    unpack = lambda t: t[:, :, :E].transpose(0, 2, 1)
    return unpack(idx), unpack(gate)


def _moe_kernel(cap, S, D, idx_ref, gate_ref, x1_hbm, h2_hbm, wg_ref, wu_ref, wd_ref, out_hbm,
                acc_ref, h2_ref, xin_ref, y_ref, slab_sem):
    n_slab = D // LANES
    seq_pitch = S + SLAB_PAD
    cap_pitch = cap + SLAB_PAD
    b = pl.program_id(0)
    e = pl.program_id(1)
    n_b = pl.num_programs(0)
    n_e = pl.num_programs(1)
    step = b * n_e + e
    last = n_b * n_e - 1

    def move(hbm, vmem_ref, seq, load):
        copies = []
        for k in range(n_slab):
            vm = vmem_ref.at[pl.ds(k * seq_pitch, S), :]
            hb = hbm.at[seq, :, pl.ds(k * LANES, LANES)]
            copies.append(pltpu.make_async_copy(hb, vm, slab_sem) if load
                          else pltpu.make_async_copy(vm, hb, slab_sem))
        for cp in copies:
            cp.start()
        for cp in copies:
            cp.wait()

    def gather_row(c, src_step):
        t = idx_ref[src_step * cap + c]
        xin_ref[pl.ds(c, n_slab, stride=cap_pitch), :] = h2_ref[pl.ds(t, n_slab, stride=seq_pitch), :]

    def scatter_rows(first, src_step, live):
        updates = []
        for i in range(SCATTER_BATCH):
            t = idx_ref[src_step * cap + first + i]
            g = gate_ref[src_step * cap + first + i] * live
            w = y_ref[pl.ds(first + i, n_slab, stride=cap_pitch), :] * g
            updates.append((t, acc_ref[pl.ds(t, n_slab, stride=seq_pitch), :] + w))
        for t, v in updates:
            acc_ref[pl.ds(t, n_slab, stride=seq_pitch), :] = v

    @pl.when(step == 0)
    def _():
        y_ref[...] = jnp.zeros_like(y_ref)
        move(x1_hbm, acc_ref, b, True)
        move(h2_hbm, h2_ref, b, True)

        def fill(c, carry):
            gather_row(c, step)
            return carry

        lax.fori_loop(0, cap, fill, 0)

    @pl.when((e == n_e - 1) & (b + 1 < n_b))
    def _():
        move(h2_hbm, h2_ref, b + 1, True)

    xb = jnp.concatenate([xin_ref[k * cap_pitch:k * cap_pitch + cap, :] for k in range(n_slab)],
                         axis=1).astype(BF16)
    nxt = jnp.minimum(step + 1, last)
    for c in range(cap):
        gather_row(c, nxt)
    a = jnp.dot(xb, wg_ref[...], preferred_element_type=F32)
    u = jnp.dot(xb, wu_ref[...], preferred_element_type=F32)
    hm = (a * (1.0 / (1.0 + jnp.exp(-a))) * u).astype(BF16)
    y = jnp.dot(hm, wd_ref[...], preferred_element_type=F32)
    prev = jnp.maximum(step - 1, 0)
    live = (step > 0).astype(F32)
    for first in range(0, cap, SCATTER_BATCH):
        scatter_rows(first, prev, live)
    for k in range(n_slab):
        y_ref[k * cap_pitch:k * cap_pitch + cap, :] = y[:, k * LANES:(k + 1) * LANES]

    @pl.when((e == 0) & (b > 0))
    def _():
        move(out_hbm, acc_ref, b - 1, False)
        move(x1_hbm, acc_ref, b, True)

    @pl.when(step == last)
    def _():
        def drain(j, carry):
            scatter_rows(j * SCATTER_BATCH, step, jnp.float32(1.0))
            return carry

        lax.fori_loop(0, cap // SCATTER_BATCH, drain, 0)
        move(out_hbm, acc_ref, b, False)


def _moe_call(idx_flat, gate_flat, x1, h2, wg, wu, wd, cap):
    B, S, D = x1.shape
    E, _, FF = wg.shape
    assert D % LANES == 0 and cap % SCATTER_BATCH == 0
    n_slab = D // LANES
    smem = pl.BlockSpec(memory_space=pltpu.SMEM)
    hbm = pl.BlockSpec(memory_space=pl.ANY)
    return pl.pallas_call(
        functools.partial(_moe_kernel, cap, S, D),
        grid=(B, E),
        in_specs=[smem, smem, hbm, hbm,
                  pl.BlockSpec((None, D, FF), lambda b, e: (e, 0, 0)),
                  pl.BlockSpec((None, D, FF), lambda b, e: (e, 0, 0)),
                  pl.BlockSpec((None, FF, D), lambda b, e: (e, 0, 0))],
        out_specs=hbm,
        out_shape=jax.ShapeDtypeStruct((B, S, D), F32),
        scratch_shapes=[pltpu.VMEM((n_slab * (S + SLAB_PAD), LANES), F32),
                        pltpu.VMEM((n_slab * (S + SLAB_PAD), LANES), F32),
                        pltpu.VMEM((n_slab * (cap + SLAB_PAD), LANES), F32),
                        pltpu.VMEM((n_slab * (cap + SLAB_PAD), LANES), F32),
                        pltpu.SemaphoreType.DMA(())],
        compiler_params=_cparams(("arbitrary", "arbitrary")),
        name="moe",
    )(idx_flat, gate_flat, x1, h2, wg, wu, wd)


def _rope_tables(S):
    half = HEAD_DIM // 2
    inv = ROPE_THETA ** (-jnp.arange(half, dtype=F32) / half)
    ang = jnp.arange(S, dtype=F32)[:, None] * inv[None, :]
    cos = jnp.cos(ang)
    sin = jnp.sin(ang)
    reps = LANE_CHUNK // HEAD_DIM
    return (jnp.tile(jnp.concatenate([cos, cos], axis=-1), (1, reps)),
            jnp.tile(jnp.concatenate([-sin, sin], axis=-1), (1, reps)))


def _layer(x, norm1_g, w_in, dil_q_g, dil_k_g, na_q_g, na_k_g, na_rpb, w_dil_branch, w_na_branch,
           w_out, norm2_g, w_router, w_gate, w_up, w_down):
    B, S, D = x.shape
    M = B * S
    tm = 512
    scale = HEAD_DIM ** -0.5
    dil_heads = DIL_WIDTH // HEAD_DIM
    gain_cols = jnp.concatenate([
        jnp.tile(dil_q_g * scale, dil_heads), jnp.tile(dil_k_g, dil_heads), jnp.ones((DIL_WIDTH,), F32),
        jnp.tile(na_q_g * scale, NA_HEADS), jnp.tile(na_k_g, NA_HEADS), jnp.ones((NA_WIDTH + 2 * D,), F32),
    ])[None, :].astype(F32)
    cos_t, sin_t = _rope_tables(S)
    blk = np.arange(LANE_CHUNK) // HEAD_DIM
    bd = jnp.asarray(blk[:, None] == blk[None, :], BF16)

    x2 = x.reshape(M, D)
    proj = _proj_call(x2, norm1_g[None, :], w_in.astype(BF16), gain_cols, cos_t, sin_t, bd, S, D, tm)
    n_groups = len(DIL_PAIRS)
    qa, ka, va = proj[:n_groups], proj[n_groups:2 * n_groups], proj[2 * n_groups:3 * n_groups]
    qn, kn, vn, ga, gn = proj[3 * n_groups:]

    o_parts, st_parts = [], []
    for g, (window, dilation) in enumerate(DIL_PAIRS):
        o, st = _dilated_call(qa[g], ka[g], va[g], B, S, g, window, dilation)
        o_parts.append(o)
        st_parts.append(st)

    yb = _na_call(qn.reshape(B, S, -1), kn.reshape(B, S, -1), vn.reshape(B, S, -1),
                  _na_bias_table(na_rpb)).reshape(M, NA_WIDTH)

    wr_t = w_router.T
    wr_hi = wr_t.astype(BF16)
    wr_lo = (wr_t - wr_hi.astype(F32)).astype(BF16)
    x1, h2, aff = _mix_call(x2, o_parts, st_parts, yb, ga, gn,
                            w_dil_branch.astype(BF16), w_na_branch.astype(BF16), w_out.astype(BF16),
                            norm2_g[None, :], wr_hi, wr_lo, B, S, 2 * MIX_SUB_ROWS)

    cap = (EC_CAPACITY_FACTOR * S) // N_EXPERTS
    idx, gates = _select_call(aff, cap)
    return _moe_call(idx.reshape(-1), gates.reshape(-1),
                     x1.reshape(B, S, D), h2.reshape(B, S, D),
                     w_gate.astype(BF16), w_up.astype(BF16), w_down.astype(BF16), cap)


def kernel(x, norm1_g, w_in, dil_q_norm_g, dil_k_norm_g, na_q_norm_g, na_k_norm_g, na_rpb, w_dil_branch,
           w_na_branch, w_out, norm2_g, w_router, w_gate, w_up, w_down):
    for l in range(norm1_g.shape[0]):
        x = _layer(x, norm1_g[l], w_in[l], dil_q_norm_g[l], dil_k_norm_g[l], na_q_norm_g[l],
                   na_k_norm_g[l], na_rpb[l], w_dil_branch[l], w_na_branch[l], w_out[l], norm2_g[l],
                   w_router[l], w_gate[l], w_up[l], w_down[l])
    return x
```

```python
import functools
import math

import jax
import jax.numpy as jnp
import numpy as np
from jax import lax
from jax.experimental import pallas as pl
from jax.experimental.pallas import tpu as pltpu

F32 = jnp.float32
BF16 = jnp.bfloat16

HEAD_DIM = 64
DIL_PAIRS = ((128, 1), (512, 4), (2048, 16))
DIL_GROUP_HEADS = 4
DIL_WIDTH = DIL_GROUP_HEADS * len(DIL_PAIRS) * HEAD_DIM
DIL_OUT = DIL_GROUP_HEADS * HEAD_DIM
NA_HEADS = 8
NA_WIDTH = NA_HEADS * HEAD_DIM
GRID_W = 64
NA_KH = 8
NA_KW = 16
N_EXPERTS = 16
EC_CAPACITY_FACTOR = 2
ROPE_THETA = 10000.0
EPS = 1e-6
MASKED = -1e30

LANE_CHUNK = 256
QUAD = 4 * HEAD_DIM
STAT_LANES = 128
LANES = 128
SLAB_PAD = 8
SCATTER_BATCH = 8
DIL_RESIDUES_PER_STEP = 4
MIX_TILE_ROWS = 1024
MIX_SUB_ROWS = 512
VMEM_LIMIT = 56 * 1024 * 1024


def _cparams(sem):
    return pltpu.CompilerParams(dimension_semantics=sem, vmem_limit_bytes=VMEM_LIMIT)


def _const_spec(shape):
    nd = len(shape)
    return pl.BlockSpec(shape, lambda *_: (0,) * nd, pipeline_mode=pl.Buffered(1))


def _proj_kernel(segs, x_ref, g1_ref, w_ref, gain_ref, cos_ref, sin_ref, bd_ref, *out_and_scratch):
    *out_refs, tmp_ref = out_and_scratch
    x = x_ref[...]
    ms = jnp.mean(x * x, axis=-1, keepdims=True)
    h = (x * lax.rsqrt(ms + EPS) * g1_ref[...]).astype(BF16)
    bd = bd_ref[...]
    lane = lax.broadcasted_iota(jnp.int32, (1, LANE_CHUNK), 1)
    first_half = (lane % HEAD_DIM) < (HEAD_DIM // 2)

    def head_norm(p, c0):
        ss = jnp.dot((p * p).astype(BF16), bd, preferred_element_type=F32)
        return p * lax.rsqrt(ss * (1.0 / HEAD_DIM) + EPS) * gain_ref[:, c0:c0 + LANE_CHUNK]

    def rope(y):
        partner = jnp.where(first_half,
                            pltpu.roll(y, LANE_CHUNK - HEAD_DIM // 2, 1),
                            pltpu.roll(y, HEAD_DIM // 2, 1))
        return y * cos_ref[...] + partner * sin_ref[...]

    tm = x.shape[0]
    col = 0
    for ref, (width, kind, dil) in zip(out_refs, segs):
        for c in range(0, width, LANE_CHUNK):
            p = jnp.dot(h, w_ref[:, col + c:col + c + LANE_CHUNK], preferred_element_type=F32)
            if kind == "qk_rope":
                p = rope(head_norm(p, col + c))
            elif kind == "qk":
                p = head_norm(p, col + c)
            elif kind == "gate":
                p = 1.0 / (1.0 + jnp.exp(-p))
            if dil == 1:
                ref[:, c:c + LANE_CHUNK] = p.astype(ref.dtype)
            else:
                for half in range(LANE_CHUNK // LANES):
                    tmp_ref[half] = p[:, half * LANES:(half + 1) * LANES]
                for r in range(dil):
                    for half in range(LANE_CHUNK // LANES):
                        c0 = r * LANE_CHUNK + half * LANES
                        ref[:, c0:c0 + LANES] = tmp_ref[half, pl.ds(r, tm // dil, stride=dil), :].astype(ref.dtype)
        col += width


def _proj_call(x2, g1, w_in, gain_cols, cos_t, sin_t, bd, S, d_model, tm):
    M = x2.shape[0]
    dils = [d for _, d in DIL_PAIRS]
    segs = tuple((QUAD, kind, d) for kind in ("qk_rope", "qk_rope", "v") for d in dils) + (
        (NA_WIDTH, "qk", 1), (NA_WIDTH, "qk", 1), (NA_WIDTH, "v", 1),
        (d_model, "gate", 1), (d_model, "gate", 1))
    n_cols = sum(w for w, _, _ in segs)
    pos_blocks = S // tm
    row = lambda i: (i, 0)

    return pl.pallas_call(
        functools.partial(_proj_kernel, segs),
        grid=(M // tm,),
        in_specs=[
            pl.BlockSpec((tm, d_model), row),
            _const_spec((1, d_model)),
            _const_spec((d_model, n_cols)),
            _const_spec((1, n_cols)),
            pl.BlockSpec((tm, LANE_CHUNK), lambda i: (i % pos_blocks, 0)),
            pl.BlockSpec((tm, LANE_CHUNK), lambda i: (i % pos_blocks, 0)),
            _const_spec((LANE_CHUNK, LANE_CHUNK)),
        ],
        out_specs=[pl.BlockSpec((tm // d, w * d), row) for w, _, d in segs],
        out_shape=[jax.ShapeDtypeStruct((M // d, w * d), BF16) for w, _, d in segs],
        scratch_shapes=[pltpu.VMEM((LANE_CHUNK // LANES, tm, LANES), F32)],
        compiler_params=_cparams(("parallel",)),
        name="proj",
    )(x2, g1, w_in, gain_cols, cos_t, sin_t, bd)


def _stack_heads(q, lane_head):
    zero = jnp.zeros_like(q)
    return jnp.concatenate([jnp.where(lane_head == h, q, zero) for h in range(4)], axis=0)


def _pick_heads(parts, lane_head):
    out = parts[3]
    for h in (2, 1, 0):
        out = jnp.where(lane_head == h, parts[h], out)
    return out


def _dil_kernel(L, T, R, n_res, q_ref, k_ref, v_ref, o_ref, st_ref):
    KW = T + 2 * R
    lane_head = lax.broadcasted_iota(jnp.int32, (1, QUAD), 1) // HEAD_DIM
    stat_lane = lax.broadcasted_iota(jnp.int32, (1, STAT_LANES), 1)
    qi = lax.broadcasted_iota(jnp.int32, (T, KW), 0)
    kj = lax.broadcasted_iota(jnp.int32, (T, KW), 1)

    def body(i, carry):
        q0 = pl.multiple_of(i * T, T)
        ks = pl.multiple_of(jnp.clip(q0 - R, 0, L - KW), R)
        off = kj - qi + (ks - q0)
        bias = jnp.where(jnp.abs(off) <= R, 0.0, MASKED).astype(F32)
        bias = jnp.concatenate([bias] * 4, axis=0)
        for res in range(n_res):
            block(q0, ks, bias, slice(res * QUAD, (res + 1) * QUAD),
                  slice(res * STAT_LANES, (res + 1) * STAT_LANES))
        return carry

    def block(q0, ks, bias, lanes, stat_lanes):
        q = q_ref[pl.ds(q0, T), lanes]
        k = k_ref[pl.ds(ks, KW), lanes]
        v = v_ref[pl.ds(ks, KW), lanes]
        s = lax.dot_general(_stack_heads(q, lane_head), k, (((1,), (1,)), ((), ())),
                            preferred_element_type=F32)
        s = s + bias
        m = jnp.max(s, axis=-1, keepdims=True)
        p = jnp.exp(s - m)
        den = jnp.sum(p, axis=-1, keepdims=True)
        o = jnp.dot(p.astype(BF16), v, preferred_element_type=F32) / den
        stat = jnp.zeros((T, STAT_LANES), F32)
        for h in range(4):
            stat = jnp.where(stat_lane == h, m[h * T:(h + 1) * T], stat)
            stat = jnp.where(stat_lane == 4 + h, den[h * T:(h + 1) * T], stat)
        outs = [o[h * T:(h + 1) * T] for h in range(4)]
        o_ref[pl.ds(q0, T), lanes] = _pick_heads(outs, lane_head).astype(o_ref.dtype)
        st_ref[pl.ds(q0, T), stat_lanes] = stat

    lax.fori_loop(0, L // T, body, 0, unroll=max(1, 4 // n_res))


def _dilated_call(q, k, v, B, S, group, window, dilation, T=128):
    d = dilation
    R = (window // 2) // d
    L = S // d
    assert L % T == 0 and L >= T + 2 * R and R % 16 == 0
    view = lambda t: t.reshape(B, L, d * QUAD)
    n_res = min(d, DIL_RESIDUES_PER_STEP)
    in_spec = pl.BlockSpec((None, L, n_res * QUAD), lambda b, r: (b, 0, r))
    o, st = pl.pallas_call(
        functools.partial(_dil_kernel, L, T, R, n_res),
        grid=(B, d // n_res),
        in_specs=[in_spec, in_spec, in_spec],
        out_specs=[pl.BlockSpec((None, L, n_res * QUAD), lambda b, r: (b, 0, r)),
                   pl.BlockSpec((None, L, n_res * STAT_LANES), lambda b, r: (b, 0, r))],
        out_shape=[jax.ShapeDtypeStruct((B, L, d * QUAD), BF16),
                   jax.ShapeDtypeStruct((B, L, d * STAT_LANES), F32)],
        compiler_params=_cparams(("parallel", "parallel")),
        name=f"dilated_g{group}",
    )(view(q), view(k), view(v))
    return o.reshape(B * L, d * QUAD), st.reshape(B * L, d * STAT_LANES)


def _na_kernel(rows, RB, q_ref, k_ref, v_ref, bias_ref, o_ref):
    lane_head = lax.broadcasted_iota(jnp.int32, (1, QUAD), 1) // HEAD_DIM
    n_keys = NA_KH * GRID_W
    i = pl.program_id(1)
    for rr in range(RB):
        r = i * RB + rr
        rs = jnp.clip(r - NA_KH // 2, 0, rows - NA_KH)
        e = rs - r + NA_KH - 1
        k0 = pl.multiple_of(rs * GRID_W, GRID_W)
        for quad in range(NA_WIDTH // QUAD):
            lanes = slice(quad * QUAD, (quad + 1) * QUAD)
            q = q_ref[rr * GRID_W:(rr + 1) * GRID_W, lanes]
            k = k_ref[pl.ds(k0, n_keys), lanes]
            v = v_ref[pl.ds(k0, n_keys), lanes]
            s = lax.dot_general(_stack_heads(q, lane_head), k, (((1,), (1,)), ((), ())),
                                preferred_element_type=F32)
            s = s + bias_ref[quad, e]
            m = jnp.max(s, axis=-1, keepdims=True)
            p = jnp.exp(s - m)
            den = jnp.sum(p, axis=-1, keepdims=True)
            o = jnp.dot(p.astype(BF16), v, preferred_element_type=F32) / den
            outs = [o[h * GRID_W:(h + 1) * GRID_W] for h in range(4)]
            o_ref[rr * GRID_W:(rr + 1) * GRID_W, lanes] = _pick_heads(outs, lane_head).astype(o_ref.dtype)


def _na_bias_table(rpb):
    c = np.arange(GRID_W)
    ws = np.clip(c - NA_KW // 2, 0, GRID_W - NA_KW)
    col_ok = (c[None, :] >= ws[:, None]) & (c[None, :] < ws[:, None] + NA_KW)
    dc = np.clip(c[None, :] - c[:, None] + NA_KW - 1, 0, 2 * NA_KW - 2)
    n_dc = 2 * NA_KW - 1
    n_heads, n_dr = rpb.shape[0], rpb.shape[1]
    onehot = np.zeros((n_dc, GRID_W * GRID_W), np.float32)
    onehot[dc.reshape(-1), np.arange(GRID_W * GRID_W)] = 1.0
    a = jnp.dot(rpb.reshape(n_heads * n_dr, n_dc).astype(F32), onehot, precision=lax.Precision.HIGHEST)
    a = jnp.where(col_ok[None, None], a.reshape(n_heads, n_dr, GRID_W, GRID_W), MASKED)
    t = jnp.stack([a[:, e:e + NA_KH] for e in range(NA_KH)], axis=1)
    t = t.reshape(n_heads // 4, 4, NA_KH, NA_KH, GRID_W, GRID_W).transpose(0, 2, 1, 4, 3, 5)
    return t.reshape(n_heads // 4, NA_KH, 4 * GRID_W, NA_KH * GRID_W)


def _na_call(qn, kn, vn, bias, RB=8):
    B, S, _ = qn.shape
    rows = S // GRID_W
    assert rows >= NA_KH and rows % RB == 0
    blk = pl.BlockSpec((None, RB * GRID_W, NA_WIDTH), lambda b, i: (b, i, 0))
    full = pl.BlockSpec((None, S, NA_WIDTH), lambda b, i: (b, 0, 0))
    return pl.pallas_call(
        functools.partial(_na_kernel, rows, RB),
        grid=(B, rows // RB),
        in_specs=[blk, full, full, _const_spec(bias.shape)],
        out_specs=blk,
        out_shape=jax.ShapeDtypeStruct((B, S, NA_WIDTH), BF16),
        compiler_params=_cparams(("parallel", "arbitrary")),
        name="neighbourhood",
    )(qn, kn, vn, bias)


def _token_major(o_ref, s_ref, o_tmp, s_tmp, dil):
    if dil == 1:
        return o_ref[...].astype(F32), s_ref[...]
    rows = o_ref.shape[0]
    halves = QUAD // LANES
    for r in range(dil):
        blk = o_ref[:, r * QUAD:(r + 1) * QUAD].astype(F32)
        for half in range(halves):
            o_tmp[half, pl.ds(r, rows, stride=dil), :] = blk[:, half * LANES:(half + 1) * LANES]
        s_tmp[pl.ds(r, rows, stride=dil), :] = s_ref[:, r * STAT_LANES:(r + 1) * STAT_LANES]
    return jnp.concatenate([o_tmp[half] for half in range(halves)], axis=1), s_tmp[...]


def _mix_kernel(dils, x_ref, o0_ref, o1_ref, o2_ref, s0_ref, s1_ref, s2_ref, yb_ref, ga_ref, gn_ref,
                pa_ref, pb_ref, wo_ref, g2_ref, wrh_ref, wrl_ref, x1_ref, h2_ref, aff_ref, o_tmp, s_tmp):
    tile_outs, tile_stats = [], []
    for g, (o_ref, s_ref, dil) in enumerate(zip((o0_ref, o1_ref, o2_ref), (s0_ref, s1_ref, s2_ref), dils)):
        o, st = _token_major(o_ref, s_ref, o_tmp.at[g], s_tmp.at[g], dil)
        tile_outs.append(o)
        tile_stats.append(st)
    for r0 in range(0, x_ref.shape[0], MIX_SUB_ROWS):
        rows = slice(r0, r0 + MIX_SUB_ROWS)
        _mix_rows([o[rows] for o in tile_outs], [st[rows] for st in tile_stats],
                  x_ref.at[rows], yb_ref.at[rows], ga_ref.at[rows], gn_ref.at[rows],
                  pa_ref, pb_ref, wo_ref, g2_ref, wrh_ref, wrl_ref,
                  x1_ref.at[rows], h2_ref.at[rows], aff_ref.at[:, rows])


def _mix_rows(outs, stats, x_ref, yb_ref, ga_ref, gn_ref, pa_ref, pb_ref, wo_ref, g2_ref, wrh_ref, wrl_ref,
              x1_ref, h2_ref, aff_ref):
    lane_head = lax.broadcasted_iota(jnp.int32, (1, QUAD), 1) // HEAD_DIM
    coef = [None, None, None]
    for h in range(4):
        ms = [st[:, h:h + 1] for st in stats]
        dens = [st[:, 4 + h:5 + h] for st in stats]
        top = jnp.maximum(jnp.maximum(ms[0], ms[1]), ms[2])
        ws = [dens[g] * jnp.exp(ms[g] - top) for g in range(3)]
        tot = ws[0] + ws[1] + ws[2]
        for g in range(3):
            cg = ws[g] / tot
            coef[g] = cg if coef[g] is None else jnp.where(lane_head == h, cg, coef[g])
    ya = coef[0] * outs[0] + coef[1] * outs[1] + coef[2] * outs[2]
    ta = jnp.dot(ya.astype(BF16), pa_ref[...], preferred_element_type=F32)
    tb = jnp.dot(yb_ref[...], pb_ref[...], preferred_element_type=F32)
    merged = ga_ref[...].astype(F32) * ta + gn_ref[...].astype(F32) * tb
    x1 = x_ref[...] + jnp.dot(merged.astype(BF16), wo_ref[...], preferred_element_type=F32)
    x1_ref[...] = x1
    ms2 = jnp.mean(x1 * x1, axis=-1, keepdims=True)
    h2 = x1 * lax.rsqrt(ms2 + EPS) * g2_ref[...]
    h2_ref[...] = h2
    h_hi = h2.astype(BF16)
    h_lo = (h2 - h_hi.astype(F32)).astype(BF16)
    nt = (((1,), (1,)), ((), ()))
    logits = (lax.dot_general(wrh_ref[...], h_hi, nt, preferred_element_type=F32)
              + lax.dot_general(wrh_ref[...], h_lo, nt, preferred_element_type=F32)
              + lax.dot_general(wrl_ref[...], h_hi, nt, preferred_element_type=F32))
    z = jnp.exp(logits - jnp.max(logits, axis=0, keepdims=True))
    aff_ref[...] = z / jnp.sum(z, axis=0, keepdims=True)


def _mix_call(x2, o_parts, st_parts, yb, ga, gn, pa, pb, wo, g2, wr_hi, wr_lo, B, S, tm):
    M, D = x2.shape
    row = lambda i: (i, 0)
    blocks = S // tm
    rows_of = lambda w, d=1: pl.BlockSpec((tm // d, w * d), row)
    dils = tuple(d for _, d in DIL_PAIRS)
    return pl.pallas_call(
        functools.partial(_mix_kernel, dils),
        grid=(M // tm,),
        in_specs=[rows_of(D)] + [rows_of(QUAD, d) for d in dils] + [rows_of(STAT_LANES, d) for d in dils]
                 + [rows_of(NA_WIDTH), rows_of(D), rows_of(D),
                    _const_spec(pa.shape), _const_spec(pb.shape), _const_spec(wo.shape),
                    _const_spec(g2.shape), _const_spec(wr_hi.shape), _const_spec(wr_lo.shape)],
        out_specs=[rows_of(D), rows_of(D),
                   pl.BlockSpec((None, N_EXPERTS, tm), lambda i: (i // blocks, 0, i % blocks))],
        out_shape=[jax.ShapeDtypeStruct((M, D), F32), jax.ShapeDtypeStruct((M, D), F32),
                   jax.ShapeDtypeStruct((B, N_EXPERTS, S), F32)],
        scratch_shapes=[pltpu.VMEM((len(dils), QUAD // LANES, tm, LANES), F32),
                        pltpu.VMEM((len(dils), tm, LANES), F32)],
        compiler_params=_cparams(("parallel",)),
        name="mix",
    )(x2, *o_parts, *st_parts, yb, ga, gn, pa, pb, wo, g2, wr_hi, wr_lo)


def _select_kernel(cap, n_exp, rows, aff_ref, tri_ref, below_ref, idx_ref, gate_ref):
    n_bits = 31
    affs = [aff_ref[e * rows:(e + 1) * rows, :] for e in range(n_exp)]
    as_float = lambda word: pltpu.bitcast(word, F32)

    def count(mask):
        return jnp.sum(jnp.sum(mask.astype(F32), axis=1, keepdims=True), axis=0, keepdims=True)

    def refine(i, thrs):
        bit = lax.shift_left(jnp.int32(1), n_bits - 1 - i)
        out = []
        for e in range(n_exp):
            cand = thrs[e] | bit
            out.append(jnp.where(count(affs[e] >= as_float(cand)) >= cap, cand, thrs[e]))
        return tuple(out)

    thrs = lax.fori_loop(0, n_bits, refine, tuple(jnp.zeros((1, 1), jnp.int32) for _ in range(n_exp)))

    tri = tri_ref[...]
    below = below_ref[...]

    def running_count(mask):
        in_row = jnp.dot(mask.astype(BF16), tri, preferred_element_type=F32)
        row_tot = jnp.broadcast_to(in_row[:, LANES - 1:LANES], in_row.shape).astype(BF16)
        return in_row + jnp.dot(below, row_tot, preferred_element_type=F32)

    cuts = [as_float(t) for t in thrs]
    above = jnp.concatenate([affs[e] > cuts[e] for e in range(n_exp)], axis=0)
    equal = jnp.concatenate([affs[e] == cuts[e] for e in range(n_exp)], axis=0)
    room = jnp.concatenate(
        [jnp.broadcast_to(cap - count(affs[e] > cuts[e]), (rows, LANES)) for e in range(n_exp)], axis=0)
    chosen = above | (equal & (running_count(equal) <= room))
    cnt = running_count(chosen)

    lane = lax.broadcasted_iota(jnp.int32, (1, LANES), 1)
    lane_f = lane.astype(F32)
    slot = lax.broadcasted_iota(jnp.int32, (cap, 1), 0).astype(F32)
    diag = lax.broadcasted_iota(jnp.int32, (rows, LANES), 0) == lax.broadcasted_iota(jnp.int32, (rows, LANES), 1)
    pad = jnp.zeros((LANES - rows, LANES), BF16)
    idx_out = jnp.zeros((cap, LANES), jnp.int32)
    gate_out = jnp.zeros((cap, LANES), F32)
    for e in range(n_exp):
        cnt_e = cnt[e * rows:(e + 1) * rows]
        aff_e = aff_ref[e * rows:(e + 1) * rows, :]
        row_end = jnp.sum(jnp.where(diag, cnt_e[:, LANES - 1:LANES], 0.0), axis=0, keepdims=True)
        row_end = jnp.where(lane < rows, row_end, float(2 * cap))
        row_of = jnp.sum((row_end <= slot).astype(F32), axis=1, keepdims=True)
        pick = (lane_f == row_of).astype(BF16)
        cnt_hi = jnp.floor(cnt_e * (1.0 / 256.0))
        a_hi = aff_e.astype(BF16)
        a_r1 = aff_e - a_hi.astype(F32)
        a_mid = a_r1.astype(BF16)
        a_lo = (a_r1 - a_mid.astype(F32)).astype(BF16)
        pieces = [cnt_hi.astype(BF16), (cnt_e - 256.0 * cnt_hi).astype(BF16), a_hi, a_mid, a_lo]
        rhs = jnp.concatenate([jnp.concatenate([p, pad], axis=0) for p in pieces], axis=1)
        got = jnp.dot(pick, rhs, preferred_element_type=F32)
        row_cnt = 256.0 * got[:, 0:LANES] + got[:, LANES:2 * LANES]
        row_aff = got[:, 2 * LANES:3 * LANES] + got[:, 3 * LANES:4 * LANES] + got[:, 4 * LANES:5 * LANES]
        lane_of = jnp.sum((row_cnt <= slot).astype(F32), axis=1, keepdims=True)
        gate = jnp.sum(jnp.where(lane_f == lane_of, row_aff, 0.0), axis=1, keepdims=True)
        token = (row_of * float(LANES) + lane_of).astype(jnp.int32)
        idx_out = jnp.where(lane == e, token, idx_out)
        gate_out = jnp.where(lane == e, gate, gate_out)
    idx_ref[...] = idx_out
    gate_ref[...] = gate_out


def _select_call(aff, cap):
    B, E, S = aff.shape
    rows = S // LANES
    assert S % LANES == 0 and rows <= LANES and rows % 8 == 0 and cap <= 256 * 256
    r = np.arange(E * rows)
    tri = jnp.asarray(np.arange(LANES)[:, None] <= np.arange(LANES)[None, :], BF16)
    below = jnp.asarray((r[:, None] // rows == r[None, :] // rows) & (r[None, :] < r[:, None]), BF16)
    out_spec = pl.BlockSpec((None, cap, LANES), lambda b: (b, 0, 0))
    idx, gate = pl.pallas_call(
        functools.partial(_select_kernel, cap, E, rows),
        grid=(B,),
        in_specs=[pl.BlockSpec((None, E * rows, LANES), lambda b: (b, 0, 0)),
                  _const_spec(tri.shape), _const_spec(below.shape)],
        out_specs=[out_spec, out_spec],
        out_shape=[jax.ShapeDtypeStruct((B, cap, LANES), jnp.int32),
                   jax.ShapeDtypeStruct((B, cap, LANES), F32)],
        compiler_params=_cparams(("parallel",)),
        name="select",
    )(aff.reshape(B, E * rows, LANES), tri, below)
    unpack = lambda t: t[:, :, :E].transpose(0, 2, 1)
    return unpack(idx), unpack(gate)


def _moe_kernel(cap, S, D, idx_ref, gate_ref, x1_hbm, h2_hbm, wg_ref, wu_ref, wd_ref, out_hbm,
                acc_ref, h2_ref, xin_ref, y_ref, slab_sem):
    n_slab = D // LANES
    seq_pitch = S + SLAB_PAD
    cap_pitch = cap + SLAB_PAD
    b = pl.program_id(0)
    e = pl.program_id(1)
    n_b = pl.num_programs(0)
    n_e = pl.num_programs(1)
    step = b * n_e + e
    last = n_b * n_e - 1

    def slab_copies(hbm, vmem_ref, seq, load, sem0=0):
        copies = []
        for k in range(n_slab):
            vm = vmem_ref.at[pl.ds(k * seq_pitch, S), :]
            hb = hbm.at[seq, :, pl.ds(k * LANES, LANES)]
            sem = slab_sem.at[sem0 + k]
            copies.append(pltpu.make_async_copy(hb, vm, sem) if load else pltpu.make_async_copy(vm, hb, sem))
        return copies

    def move(hbm, vmem_ref, seq, load):
        copies = slab_copies(hbm, vmem_ref, seq, load)
        for cp in copies:
            cp.start()
        for cp in copies:
            cp.wait()

    def gather_row(c, src_step):
        t = idx_ref[src_step * cap + c]
        xin_ref[pl.ds(c, n_slab, stride=cap_pitch), :] = h2_ref[pl.ds(t, n_slab, stride=seq_pitch), :]

    def scatter_rows(first, src_step, live):
        updates = []
        for i in range(SCATTER_BATCH):
            t = idx_ref[src_step * cap + first + i]
            g = gate_ref[src_step * cap + first + i] * live
            w = y_ref[pl.ds(first + i, n_slab, stride=cap_pitch), :] * g
            updates.append((t, acc_ref[pl.ds(t, n_slab, stride=seq_pitch), :] + w))
        for t, v in updates:
            acc_ref[pl.ds(t, n_slab, stride=seq_pitch), :] = v

    @pl.when(step == 0)
    def _():
        y_ref[...] = jnp.zeros_like(y_ref)
        loads = slab_copies(x1_hbm, acc_ref, b, True) + slab_copies(h2_hbm, h2_ref, b, True, sem0=n_slab)
        for cp in loads:
            cp.start()
        for cp in loads:
            cp.wait()

        def fill(c, carry):
            gather_row(c, step)
            return carry

        lax.fori_loop(0, cap, fill, 0)

    @pl.when((e == n_e - 1) & (b + 1 < n_b))
    def _():
        move(h2_hbm, h2_ref, b + 1, True)

    xb = jnp.concatenate([xin_ref[k * cap_pitch:k * cap_pitch + cap, :] for k in range(n_slab)],
                         axis=1).astype(BF16)
    nxt = jnp.minimum(step + 1, last)
    for c in range(cap):
        gather_row(c, nxt)
    a = jnp.dot(xb, wg_ref[...], preferred_element_type=F32)
    u = jnp.dot(xb, wu_ref[...], preferred_element_type=F32)
    hm = (a * (1.0 / (1.0 + jnp.exp(-a))) * u).astype(BF16)
    y = jnp.dot(hm, wd_ref[...], preferred_element_type=F32)
    prev = jnp.maximum(step - 1, 0)
    live = (step > 0).astype(F32)
    for first in range(0, cap, SCATTER_BATCH):
        scatter_rows(first, prev, live)
    for k in range(n_slab):
        y_ref[k * cap_pitch:k * cap_pitch + cap, :] = y[:, k * LANES:(k + 1) * LANES]

    @pl.when((e == 0) & (b > 0))
    def _():
        stores = slab_copies(out_hbm, acc_ref, b - 1, False)
        loads = slab_copies(x1_hbm, acc_ref, b, True, sem0=n_slab)
        for cp in stores:
            cp.start()
        for st, ld in zip(stores, loads):
            st.wait()
            ld.start()
        for cp in loads:
            cp.wait()

    @pl.when(step == last)
    def _():
        def drain(j, carry):
            scatter_rows(j * SCATTER_BATCH, step, jnp.float32(1.0))
            return carry

        lax.fori_loop(0, cap // SCATTER_BATCH, drain, 0)
        move(out_hbm, acc_ref, b, False)


def _moe_call(idx_flat, gate_flat, x1, h2, wg, wu, wd, cap):
    B, S, D = x1.shape
    E, _, FF = wg.shape
    assert D % LANES == 0 and cap % SCATTER_BATCH == 0
    n_slab = D // LANES
    smem = pl.BlockSpec(memory_space=pltpu.SMEM)
    hbm = pl.BlockSpec(memory_space=pl.ANY)
    return pl.pallas_call(
        functools.partial(_moe_kernel, cap, S, D),
        grid=(B, E),
        in_specs=[smem, smem, hbm, hbm,
                  pl.BlockSpec((None, D, FF), lambda b, e: (e, 0, 0)),
                  pl.BlockSpec((None, D, FF), lambda b, e: (e, 0, 0)),
                  pl.BlockSpec((None, FF, D), lambda b, e: (e, 0, 0))],
        out_specs=hbm,
        out_shape=jax.ShapeDtypeStruct((B, S, D), F32),
        scratch_shapes=[pltpu.VMEM((n_slab * (S + SLAB_PAD), LANES), F32),
                        pltpu.VMEM((n_slab * (S + SLAB_PAD), LANES), F32),
                        pltpu.VMEM((n_slab * (cap + SLAB_PAD), LANES), F32),
                        pltpu.VMEM((n_slab * (cap + SLAB_PAD), LANES), F32),
                        pltpu.SemaphoreType.DMA((2 * n_slab,))],
        compiler_params=_cparams(("arbitrary", "arbitrary")),
        name="moe",
    )(idx_flat, gate_flat, x1, h2, wg, wu, wd)


def _rope_tables(S):
    half = HEAD_DIM // 2
    inv = ROPE_THETA ** (-jnp.arange(half, dtype=F32) / half)
    ang = jnp.arange(S, dtype=F32)[:, None] * inv[None, :]
    cos = jnp.cos(ang)
    sin = jnp.sin(ang)
    reps = LANE_CHUNK // HEAD_DIM
    return (jnp.tile(jnp.concatenate([cos, cos], axis=-1), (1, reps)),
            jnp.tile(jnp.concatenate([-sin, sin], axis=-1), (1, reps)))


def _layer(x, norm1_g, w_in, dil_q_g, dil_k_g, na_q_g, na_k_g, na_rpb, w_dil_branch, w_na_branch,
           w_out, norm2_g, w_router, w_gate, w_up, w_down):
    B, S, D = x.shape
    M = B * S
    tm = 512
    scale = HEAD_DIM ** -0.5
    dil_heads = DIL_WIDTH // HEAD_DIM
    gain_cols = jnp.concatenate([
        jnp.tile(dil_q_g * scale, dil_heads), jnp.tile(dil_k_g, dil_heads), jnp.ones((DIL_WIDTH,), F32),
        jnp.tile(na_q_g * scale, NA_HEADS), jnp.tile(na_k_g, NA_HEADS), jnp.ones((NA_WIDTH + 2 * D,), F32),
    ])[None, :].astype(F32)
    cos_t, sin_t = _rope_tables(S)
    blk = np.arange(LANE_CHUNK) // HEAD_DIM
    bd = jnp.asarray(blk[:, None] == blk[None, :], BF16)

    x2 = x.reshape(M, D)
    proj = _proj_call(x2, norm1_g[None, :], w_in.astype(BF16), gain_cols, cos_t, sin_t, bd, S, D, tm)
    n_groups = len(DIL_PAIRS)
    qa, ka, va = proj[:n_groups], proj[n_groups:2 * n_groups], proj[2 * n_groups:3 * n_groups]
    qn, kn, vn, ga, gn = proj[3 * n_groups:]

    o_parts, st_parts = [], []
    for g, (window, dilation) in enumerate(DIL_PAIRS):
        o, st = _dilated_call(qa[g], ka[g], va[g], B, S, g, window, dilation)
        o_parts.append(o)
        st_parts.append(st)

    yb = _na_call(qn.reshape(B, S, -1), kn.reshape(B, S, -1), vn.reshape(B, S, -1),
                  _na_bias_table(na_rpb)).reshape(M, NA_WIDTH)

    wr_t = w_router.T
    wr_hi = wr_t.astype(BF16)
    wr_lo = (wr_t - wr_hi.astype(F32)).astype(BF16)
    x1, h2, aff = _mix_call(x2, o_parts, st_parts, yb, ga, gn,
                            w_dil_branch.astype(BF16), w_na_branch.astype(BF16), w_out.astype(BF16),
                            norm2_g[None, :], wr_hi, wr_lo, B, S, MIX_TILE_ROWS)

    cap = (EC_CAPACITY_FACTOR * S) // N_EXPERTS
    idx, gates = _select_call(aff, cap)
    return _moe_call(idx.reshape(-1), gates.reshape(-1),
                     x1.reshape(B, S, D), h2.reshape(B, S, D),
                     w_gate.astype(BF16), w_up.astype(BF16), w_down.astype(BF16), cap)


def kernel(x, norm1_g, w_in, dil_q_norm_g, dil_k_norm_g, na_q_norm_g, na_k_norm_g, na_rpb, w_dil_branch,
           w_na_branch, w_out, norm2_g, w_router, w_gate, w_up, w_down):
    for l in range(norm1_g.shape[0]):
        x = _layer(x, norm1_g[l], w_in[l], dil_q_norm_g[l], dil_k_norm_g[l], na_q_norm_g[l],
                   na_k_norm_g[l], na_rpb[l], w_dil_branch[l], w_na_branch[l], w_out[l], norm2_g[l],
                   w_router[l], w_gate[l], w_up[l], w_down[l])
    return x
```

```python
import functools
import math

import jax
import jax.numpy as jnp
import numpy as np
from jax import lax
from jax.experimental import pallas as pl
from jax.experimental.pallas import tpu as pltpu

F32 = jnp.float32
BF16 = jnp.bfloat16

HEAD_DIM = 64
DIL_PAIRS = ((128, 1), (512, 4), (2048, 16))
DIL_GROUP_HEADS = 4
DIL_WIDTH = DIL_GROUP_HEADS * len(DIL_PAIRS) * HEAD_DIM
DIL_OUT = DIL_GROUP_HEADS * HEAD_DIM
NA_HEADS = 8
NA_WIDTH = NA_HEADS * HEAD_DIM
GRID_W = 64
NA_KH = 8
NA_KW = 16
N_EXPERTS = 16
EC_CAPACITY_FACTOR = 2
ROPE_THETA = 10000.0
EPS = 1e-6
MASKED = -1e30

LANE_CHUNK = 256
QUAD = 4 * HEAD_DIM
STAT_LANES = 128
LANES = 128
SLAB_PAD = 8
SCATTER_BATCH = 4
DIL_RESIDUES_PER_STEP = 4
MIX_TILE_ROWS = 1024
MIX_SUB_ROWS = 512
VMEM_LIMIT = 56 * 1024 * 1024


def _cparams(sem):
    return pltpu.CompilerParams(dimension_semantics=sem, vmem_limit_bytes=VMEM_LIMIT)


def _const_spec(shape):
    nd = len(shape)
    return pl.BlockSpec(shape, lambda *_: (0,) * nd, pipeline_mode=pl.Buffered(1))


def _proj_kernel(segs, x_ref, g1_ref, w_ref, gain_ref, cos_ref, sin_ref, bd_ref, *out_and_scratch):
    *out_refs, tmp_ref = out_and_scratch
    x = x_ref[...]
    ms = jnp.mean(x * x, axis=-1, keepdims=True)
    h = (x * lax.rsqrt(ms + EPS) * g1_ref[...]).astype(BF16)
    bd = bd_ref[...]
    lane = lax.broadcasted_iota(jnp.int32, (1, LANE_CHUNK), 1)
    first_half = (lane % HEAD_DIM) < (HEAD_DIM // 2)

    def head_norm(p, c0):
        ss = jnp.dot((p * p).astype(BF16), bd, preferred_element_type=F32)
        return p * lax.rsqrt(ss * (1.0 / HEAD_DIM) + EPS) * gain_ref[:, c0:c0 + LANE_CHUNK]

    def rope(y):
        partner = jnp.where(first_half,
                            pltpu.roll(y, LANE_CHUNK - HEAD_DIM // 2, 1),
                            pltpu.roll(y, HEAD_DIM // 2, 1))
        return y * cos_ref[...] + partner * sin_ref[...]

    tm = x.shape[0]
    col = 0
    for ref, (width, kind, dil) in zip(out_refs, segs):
        for c in range(0, width, LANE_CHUNK):
            p = jnp.dot(h, w_ref[:, col + c:col + c + LANE_CHUNK], preferred_element_type=F32)
            if kind == "qk_rope":
                p = rope(head_norm(p, col + c))
            elif kind == "qk":
                p = head_norm(p, col + c)
            elif kind == "gate":
                p = 1.0 / (1.0 + jnp.exp(-p))
            if dil == 1:
                ref[:, c:c + LANE_CHUNK] = p.astype(ref.dtype)
            else:
                for half in range(LANE_CHUNK // LANES):
                    tmp_ref[half] = p[:, half * LANES:(half + 1) * LANES]
                for r in range(dil):
                    for half in range(LANE_CHUNK // LANES):
                        c0 = r * LANE_CHUNK + half * LANES
                        ref[:, c0:c0 + LANES] = tmp_ref[half, pl.ds(r, tm // dil, stride=dil), :].astype(ref.dtype)
        col += width


def _proj_call(x2, g1, w_in, gain_cols, cos_t, sin_t, bd, S, d_model, tm):
    M = x2.shape[0]
    dils = [d for _, d in DIL_PAIRS]
    segs = tuple((QUAD, kind, d) for kind in ("qk_rope", "qk_rope", "v") for d in dils) + (
        (NA_WIDTH, "qk", 1), (NA_WIDTH, "qk", 1), (NA_WIDTH, "v", 1),
        (d_model, "gate", 1), (d_model, "gate", 1))
    n_cols = sum(w for w, _, _ in segs)
    pos_blocks = S // tm
    row = lambda i: (i, 0)

    return pl.pallas_call(
        functools.partial(_proj_kernel, segs),
        grid=(M // tm,),
        in_specs=[
            pl.BlockSpec((tm, d_model), row),
            _const_spec((1, d_model)),
            _const_spec((d_model, n_cols)),
            _const_spec((1, n_cols)),
            pl.BlockSpec((tm, LANE_CHUNK), lambda i: (i % pos_blocks, 0)),
            pl.BlockSpec((tm, LANE_CHUNK), lambda i: (i % pos_blocks, 0)),
            _const_spec((LANE_CHUNK, LANE_CHUNK)),
        ],
        out_specs=[pl.BlockSpec((tm // d, w * d), row) for w, _, d in segs],
        out_shape=[jax.ShapeDtypeStruct((M // d, w * d), BF16) for w, _, d in segs],
        scratch_shapes=[pltpu.VMEM((LANE_CHUNK // LANES, tm, LANES), F32)],
        compiler_params=_cparams(("parallel",)),
        name="proj",
    )(x2, g1, w_in, gain_cols, cos_t, sin_t, bd)


def _stack_heads(q, lane_head):
    zero = jnp.zeros_like(q)
    return jnp.concatenate([jnp.where(lane_head == h, q, zero) for h in range(4)], axis=0)


def _pick_heads(parts, lane_head):
    out = parts[3]
    for h in (2, 1, 0):
        out = jnp.where(lane_head == h, parts[h], out)
    return out


def _dil_kernel(L, T, R, n_res, q_ref, k_ref, v_ref, o_ref, st_ref):
    KW = T + 2 * R
    lane_head = lax.broadcasted_iota(jnp.int32, (1, QUAD), 1) // HEAD_DIM
    stat_lane = lax.broadcasted_iota(jnp.int32, (1, STAT_LANES), 1)
    qi = lax.broadcasted_iota(jnp.int32, (T, KW), 0)
    kj = lax.broadcasted_iota(jnp.int32, (T, KW), 1)

    def body(i, carry):
        q0 = pl.multiple_of(i * T, T)
        ks = pl.multiple_of(jnp.clip(q0 - R, 0, L - KW), R)
        off = kj - qi + (ks - q0)
        bias = jnp.where(jnp.abs(off) <= R, 0.0, MASKED).astype(F32)
        bias = jnp.concatenate([bias] * 4, axis=0)
        for res in range(n_res):
            block(q0, ks, bias, slice(res * QUAD, (res + 1) * QUAD),
                  slice(res * STAT_LANES, (res + 1) * STAT_LANES))
        return carry

    def block(q0, ks, bias, lanes, stat_lanes):
        q = q_ref[pl.ds(q0, T), lanes]
        k = k_ref[pl.ds(ks, KW), lanes]
        v = v_ref[pl.ds(ks, KW), lanes]
        s = lax.dot_general(_stack_heads(q, lane_head), k, (((1,), (1,)), ((), ())),
                            preferred_element_type=F32)
        s = s + bias
        m = jnp.max(s, axis=-1, keepdims=True)
        p = jnp.exp(s - m)
        den = jnp.sum(p, axis=-1, keepdims=True)
        o = jnp.dot(p.astype(BF16), v, preferred_element_type=F32) / den
        stat = jnp.zeros((T, STAT_LANES), F32)
        for h in range(4):
            stat = jnp.where(stat_lane == h, m[h * T:(h + 1) * T], stat)
            stat = jnp.where(stat_lane == 4 + h, den[h * T:(h + 1) * T], stat)
        outs = [o[h * T:(h + 1) * T] for h in range(4)]
        o_ref[pl.ds(q0, T), lanes] = _pick_heads(outs, lane_head).astype(o_ref.dtype)
        st_ref[pl.ds(q0, T), stat_lanes] = stat

    lax.fori_loop(0, L // T, body, 0, unroll=max(1, 4 // n_res))


def _dilated_call(q, k, v, B, S, group, window, dilation, T=128):
    d = dilation
    R = (window // 2) // d
    L = S // d
    assert L % T == 0 and L >= T + 2 * R and R % 16 == 0
    view = lambda t: t.reshape(B, L, d * QUAD)
    n_res = min(d, DIL_RESIDUES_PER_STEP)
    in_spec = pl.BlockSpec((None, L, n_res * QUAD), lambda b, r: (b, 0, r))
    o, st = pl.pallas_call(
        functools.partial(_dil_kernel, L, T, R, n_res),
        grid=(B, d // n_res),
        in_specs=[in_spec, in_spec, in_spec],
        out_specs=[pl.BlockSpec((None, L, n_res * QUAD), lambda b, r: (b, 0, r)),
                   pl.BlockSpec((None, L, n_res * STAT_LANES), lambda b, r: (b, 0, r))],
        out_shape=[jax.ShapeDtypeStruct((B, L, d * QUAD), BF16),
                   jax.ShapeDtypeStruct((B, L, d * STAT_LANES), F32)],
        compiler_params=_cparams(("parallel", "parallel")),
        name=f"dilated_g{group}",
    )(view(q), view(k), view(v))
    return o.reshape(B * L, d * QUAD), st.reshape(B * L, d * STAT_LANES)


def _na_kernel(rows, RB, q_ref, k_ref, v_ref, bias_ref, o_ref):
    lane_head = lax.broadcasted_iota(jnp.int32, (1, QUAD), 1) // HEAD_DIM
    n_keys = NA_KH * GRID_W
    i = pl.program_id(1)
    for rr in range(RB):
        r = i * RB + rr
        rs = jnp.clip(r - NA_KH // 2, 0, rows - NA_KH)
        e = rs - r + NA_KH - 1
        k0 = pl.multiple_of(rs * GRID_W, GRID_W)
        for quad in range(NA_WIDTH // QUAD):
            lanes = slice(quad * QUAD, (quad + 1) * QUAD)
            q = q_ref[rr * GRID_W:(rr + 1) * GRID_W, lanes]
            k = k_ref[pl.ds(k0, n_keys), lanes]
            v = v_ref[pl.ds(k0, n_keys), lanes]
            s = lax.dot_general(_stack_heads(q, lane_head), k, (((1,), (1,)), ((), ())),
                                preferred_element_type=F32)
            s = s + bias_ref[quad, e]
            m = jnp.max(s, axis=-1, keepdims=True)
            p = jnp.exp(s - m)
            den = jnp.sum(p, axis=-1, keepdims=True)
            o = jnp.dot(p.astype(BF16), v, preferred_element_type=F32) / den
            outs = [o[h * GRID_W:(h + 1) * GRID_W] for h in range(4)]
            o_ref[rr * GRID_W:(rr + 1) * GRID_W, lanes] = _pick_heads(outs, lane_head).astype(o_ref.dtype)


def _na_bias_table(rpb):
    c = np.arange(GRID_W)
    ws = np.clip(c - NA_KW // 2, 0, GRID_W - NA_KW)
    col_ok = (c[None, :] >= ws[:, None]) & (c[None, :] < ws[:, None] + NA_KW)
    dc = np.clip(c[None, :] - c[:, None] + NA_KW - 1, 0, 2 * NA_KW - 2)
    n_dc = 2 * NA_KW - 1
    n_heads, n_dr = rpb.shape[0], rpb.shape[1]
    onehot = np.zeros((n_dc, GRID_W * GRID_W), np.float32)
    onehot[dc.reshape(-1), np.arange(GRID_W * GRID_W)] = 1.0
    a = jnp.dot(rpb.reshape(n_heads * n_dr, n_dc).astype(F32), onehot, precision=lax.Precision.HIGHEST)
    a = jnp.where(col_ok[None, None], a.reshape(n_heads, n_dr, GRID_W, GRID_W), MASKED)
    t = jnp.stack([a[:, e:e + NA_KH] for e in range(NA_KH)], axis=1)
    t = t.reshape(n_heads // 4, 4, NA_KH, NA_KH, GRID_W, GRID_W).transpose(0, 2, 1, 4, 3, 5)
    return t.reshape(n_heads // 4, NA_KH, 4 * GRID_W, NA_KH * GRID_W)


def _na_call(qn, kn, vn, bias, RB=8):
    B, S, _ = qn.shape
    rows = S // GRID_W
    assert rows >= NA_KH and rows % RB == 0
    blk = pl.BlockSpec((None, RB * GRID_W, NA_WIDTH), lambda b, i: (b, i, 0))
    full = pl.BlockSpec((None, S, NA_WIDTH), lambda b, i: (b, 0, 0))
    return pl.pallas_call(
        functools.partial(_na_kernel, rows, RB),
        grid=(B, rows // RB),
        in_specs=[blk, full, full, _const_spec(bias.shape)],
        out_specs=blk,
        out_shape=jax.ShapeDtypeStruct((B, S, NA_WIDTH), BF16),
        compiler_params=_cparams(("parallel", "arbitrary")),
        name="neighbourhood",
    )(qn, kn, vn, bias)


def _token_major(o_ref, s_ref, o_tmp, s_tmp, dil):
    if dil == 1:
        return o_ref[...].astype(F32), s_ref[...]
    rows = o_ref.shape[0]
    halves = QUAD // LANES
    for r in range(dil):
        blk = o_ref[:, r * QUAD:(r + 1) * QUAD].astype(F32)
        for half in range(halves):
            o_tmp[half, pl.ds(r, rows, stride=dil), :] = blk[:, half * LANES:(half + 1) * LANES]
        s_tmp[pl.ds(r, rows, stride=dil), :] = s_ref[:, r * STAT_LANES:(r + 1) * STAT_LANES]
    return jnp.concatenate([o_tmp[half] for half in range(halves)], axis=1), s_tmp[...]


def _mix_kernel(dils, x_ref, o0_ref, o1_ref, o2_ref, s0_ref, s1_ref, s2_ref, yb_ref, ga_ref, gn_ref,
                pa_ref, pb_ref, wo_ref, g2_ref, wrh_ref, wrl_ref, x1_ref, h2_ref, aff_ref, o_tmp, s_tmp):
    tile_outs, tile_stats = [], []
    for g, (o_ref, s_ref, dil) in enumerate(zip((o0_ref, o1_ref, o2_ref), (s0_ref, s1_ref, s2_ref), dils)):
        o, st = _token_major(o_ref, s_ref, o_tmp.at[g], s_tmp.at[g], dil)
        tile_outs.append(o)
        tile_stats.append(st)
    for r0 in range(0, x_ref.shape[0], MIX_SUB_ROWS):
        rows = slice(r0, r0 + MIX_SUB_ROWS)
        _mix_rows([o[rows] for o in tile_outs], [st[rows] for st in tile_stats],
                  x_ref.at[rows], yb_ref.at[rows], ga_ref.at[rows], gn_ref.at[rows],
                  pa_ref, pb_ref, wo_ref, g2_ref, wrh_ref, wrl_ref,
                  x1_ref.at[rows], h2_ref.at[rows], aff_ref.at[:, rows])


def _mix_rows(outs, stats, x_ref, yb_ref, ga_ref, gn_ref, pa_ref, pb_ref, wo_ref, g2_ref, wrh_ref, wrl_ref,
              x1_ref, h2_ref, aff_ref):
    lane_head = lax.broadcasted_iota(jnp.int32, (1, QUAD), 1) // HEAD_DIM
    coef = [None, None, None]
    for h in range(4):
        ms = [st[:, h:h + 1] for st in stats]
        dens = [st[:, 4 + h:5 + h] for st in stats]
        top = jnp.maximum(jnp.maximum(ms[0], ms[1]), ms[2])
        ws = [dens[g] * jnp.exp(ms[g] - top) for g in range(3)]
        tot = ws[0] + ws[1] + ws[2]
        for g in range(3):
            cg = ws[g] / tot
            coef[g] = cg if coef[g] is None else jnp.where(lane_head == h, cg, coef[g])
    ya = coef[0] * outs[0] + coef[1] * outs[1] + coef[2] * outs[2]
    ta = jnp.dot(ya.astype(BF16), pa_ref[...], preferred_element_type=F32)
    tb = jnp.dot(yb_ref[...], pb_ref[...], preferred_element_type=F32)
    merged = ga_ref[...].astype(F32) * ta + gn_ref[...].astype(F32) * tb
    x1 = x_ref[...] + jnp.dot(merged.astype(BF16), wo_ref[...], preferred_element_type=F32)
    x1_ref[...] = x1
    ms2 = jnp.mean(x1 * x1, axis=-1, keepdims=True)
    h2 = x1 * lax.rsqrt(ms2 + EPS) * g2_ref[...]
    h2_ref[...] = h2
    h_hi = h2.astype(BF16)
    h_lo = (h2 - h_hi.astype(F32)).astype(BF16)
    nt = (((1,), (1,)), ((), ()))
    logits = (lax.dot_general(wrh_ref[...], h_hi, nt, preferred_element_type=F32)
              + lax.dot_general(wrh_ref[...], h_lo, nt, preferred_element_type=F32)
              + lax.dot_general(wrl_ref[...], h_hi, nt, preferred_element_type=F32))
    z = jnp.exp(logits - jnp.max(logits, axis=0, keepdims=True))
    aff_ref[...] = z / jnp.sum(z, axis=0, keepdims=True)


def _mix_call(x2, o_parts, st_parts, yb, ga, gn, pa, pb, wo, g2, wr_hi, wr_lo, B, S, tm):
    M, D = x2.shape
    row = lambda i: (i, 0)
    blocks = S // tm
    rows_of = lambda w, d=1: pl.BlockSpec((tm // d, w * d), row)
    dils = tuple(d for _, d in DIL_PAIRS)
    return pl.pallas_call(
        functools.partial(_mix_kernel, dils),
        grid=(M // tm,),
        in_specs=[rows_of(D)] + [rows_of(QUAD, d) for d in dils] + [rows_of(STAT_LANES, d) for d in dils]
                 + [rows_of(NA_WIDTH), rows_of(D), rows_of(D),
                    _const_spec(pa.shape), _const_spec(pb.shape), _const_spec(wo.shape),
                    _const_spec(g2.shape), _const_spec(wr_hi.shape), _const_spec(wr_lo.shape)],
        out_specs=[rows_of(D), rows_of(D),
                   pl.BlockSpec((None, N_EXPERTS, tm), lambda i: (i // blocks, 0, i % blocks))],
        out_shape=[jax.ShapeDtypeStruct((M, D), F32), jax.ShapeDtypeStruct((M, D), F32),
                   jax.ShapeDtypeStruct((B, N_EXPERTS, S), F32)],
        scratch_shapes=[pltpu.VMEM((len(dils), QUAD // LANES, tm, LANES), F32),
                        pltpu.VMEM((len(dils), tm, LANES), F32)],
        compiler_params=_cparams(("parallel",)),
        name="mix",
    )(x2, *o_parts, *st_parts, yb, ga, gn, pa, pb, wo, g2, wr_hi, wr_lo)


def _select_kernel(cap, n_exp, rows, aff_ref, tri_ref, below_ref, idx_ref, gate_ref):
    n_bits = 31
    affs = [aff_ref[e * rows:(e + 1) * rows, :] for e in range(n_exp)]
    as_float = lambda word: pltpu.bitcast(word, F32)

    def count(mask):
        return jnp.sum(jnp.sum(mask.astype(F32), axis=1, keepdims=True), axis=0, keepdims=True)

    def refine(i, thrs):
        bit = lax.shift_left(jnp.int32(1), n_bits - 1 - i)
        out = []
        for e in range(n_exp):
            cand = thrs[e] | bit
            out.append(jnp.where(count(affs[e] >= as_float(cand)) >= cap, cand, thrs[e]))
        return tuple(out)

    thrs = lax.fori_loop(0, n_bits, refine, tuple(jnp.zeros((1, 1), jnp.int32) for _ in range(n_exp)))

    tri = tri_ref[...]
    below = below_ref[...]

    def running_count(mask):
        in_row = jnp.dot(mask.astype(BF16), tri, preferred_element_type=F32)
        row_tot = jnp.broadcast_to(in_row[:, LANES - 1:LANES], in_row.shape).astype(BF16)
        return in_row + jnp.dot(below, row_tot, preferred_element_type=F32)

    cuts = [as_float(t) for t in thrs]
    above = jnp.concatenate([affs[e] > cuts[e] for e in range(n_exp)], axis=0)
    equal = jnp.concatenate([affs[e] == cuts[e] for e in range(n_exp)], axis=0)
    room = jnp.concatenate(
        [jnp.broadcast_to(cap - count(affs[e] > cuts[e]), (rows, LANES)) for e in range(n_exp)], axis=0)
    chosen = above | (equal & (running_count(equal) <= room))
    cnt = running_count(chosen)

    lane = lax.broadcasted_iota(jnp.int32, (1, LANES), 1)
    lane_f = lane.astype(F32)
    slot = lax.broadcasted_iota(jnp.int32, (cap, 1), 0).astype(F32)
    diag = lax.broadcasted_iota(jnp.int32, (rows, LANES), 0) == lax.broadcasted_iota(jnp.int32, (rows, LANES), 1)
    pad = jnp.zeros((LANES - rows, LANES), BF16)
    idx_out = jnp.zeros((cap, LANES), jnp.int32)
    gate_out = jnp.zeros((cap, LANES), F32)
    for e in range(n_exp):
        cnt_e = cnt[e * rows:(e + 1) * rows]
        aff_e = aff_ref[e * rows:(e + 1) * rows, :]
        row_end = jnp.sum(jnp.where(diag, cnt_e[:, LANES - 1:LANES], 0.0), axis=0, keepdims=True)
        row_end = jnp.where(lane < rows, row_end, float(2 * cap))
        row_of = jnp.sum((row_end <= slot).astype(F32), axis=1, keepdims=True)
        pick = (lane_f == row_of).astype(BF16)
        cnt_hi = jnp.floor(cnt_e * (1.0 / 256.0))
        a_hi = aff_e.astype(BF16)
        a_r1 = aff_e - a_hi.astype(F32)
        a_mid = a_r1.astype(BF16)
        a_lo = (a_r1 - a_mid.astype(F32)).astype(BF16)
        pieces = [cnt_hi.astype(BF16), (cnt_e - 256.0 * cnt_hi).astype(BF16), a_hi, a_mid, a_lo]
        rhs = jnp.concatenate([jnp.concatenate([p, pad], axis=0) for p in pieces], axis=1)
        got = jnp.dot(pick, rhs, preferred_element_type=F32)
        row_cnt = 256.0 * got[:, 0:LANES] + got[:, LANES:2 * LANES]
        row_aff = got[:, 2 * LANES:3 * LANES] + got[:, 3 * LANES:4 * LANES] + got[:, 4 * LANES:5 * LANES]
        lane_of = jnp.sum((row_cnt <= slot).astype(F32), axis=1, keepdims=True)
        gate = jnp.sum(jnp.where(lane_f == lane_of, row_aff, 0.0), axis=1, keepdims=True)
        token = (row_of * float(LANES) + lane_of).astype(jnp.int32)
        idx_out = jnp.where(lane == e, token, idx_out)
        gate_out = jnp.where(lane == e, gate, gate_out)
    idx_ref[...] = idx_out
    gate_ref[...] = gate_out


def _select_call(aff, cap):
    B, E, S = aff.shape
    rows = S // LANES
    assert S % LANES == 0 and rows <= LANES and rows % 8 == 0 and cap <= 256 * 256
    r = np.arange(E * rows)
    tri = jnp.asarray(np.arange(LANES)[:, None] <= np.arange(LANES)[None, :], BF16)
    below = jnp.asarray((r[:, None] // rows == r[None, :] // rows) & (r[None, :] < r[:, None]), BF16)
    out_spec = pl.BlockSpec((None, cap, LANES), lambda b: (b, 0, 0))
    idx, gate = pl.pallas_call(
        functools.partial(_select_kernel, cap, E, rows),
        grid=(B,),
        in_specs=[pl.BlockSpec((None, E * rows, LANES), lambda b: (b, 0, 0)),
                  _const_spec(tri.shape), _const_spec(below.shape)],
        out_specs=[out_spec, out_spec],
        out_shape=[jax.ShapeDtypeStruct((B, cap, LANES), jnp.int32),
                   jax.ShapeDtypeStruct((B, cap, LANES), F32)],
        compiler_params=_cparams(("parallel",)),
        name="select",
    )(aff.reshape(B, E * rows, LANES), tri, below)
    unpack = lambda t: t[:, :, :E].transpose(0, 2, 1)
    return unpack(idx), unpack(gate)


def _moe_kernel(cap, S, D, idx_ref, gate_ref, x1_hbm, h2_hbm, wg_ref, wu_ref, wd_ref, out_hbm,
                acc_ref, h2_ref, xin_even_ref, xin_odd_ref, y_ref, slab_sem):
    n_slab = D // LANES
    seq_pitch = S + SLAB_PAD
    cap_pitch = cap + SLAB_PAD
    b = pl.program_id(0)
    e = pl.program_id(1)
    n_b = pl.num_programs(0)
    n_e = pl.num_programs(1)
    step = b * n_e + e
    last = n_b * n_e - 1

    def slab_copies(hbm, vmem_ref, seq, load, sem0=0):
        copies = []
        for k in range(n_slab):
            vm = vmem_ref.at[pl.ds(k * seq_pitch, S), :]
            hb = hbm.at[seq, :, pl.ds(k * LANES, LANES)]
            sem = slab_sem.at[sem0 + k]
            copies.append(pltpu.make_async_copy(hb, vm, sem) if load else pltpu.make_async_copy(vm, hb, sem))
        return copies

    def move(hbm, vmem_ref, seq, load):
        copies = slab_copies(hbm, vmem_ref, seq, load)
        for cp in copies:
            cp.start()
        for cp in copies:
            cp.wait()

    def gather_row(c, src_step, xin_ref):
        t = idx_ref[src_step * cap + c]
        xin_ref[pl.ds(c, n_slab, stride=cap_pitch), :] = h2_ref[pl.ds(t, n_slab, stride=seq_pitch), :]

    def scatter_rows(first, src_step, live):
        updates = []
        for i in range(SCATTER_BATCH):
            t = idx_ref[src_step * cap + first + i]
            g = gate_ref[src_step * cap + first + i] * live
            w = y_ref[pl.ds(first + i, n_slab, stride=cap_pitch), :] * g
            updates.append((t, acc_ref[pl.ds(t, n_slab, stride=seq_pitch), :] + w))
        for t, v in updates:
            acc_ref[pl.ds(t, n_slab, stride=seq_pitch), :] = v

    @pl.when(step == 0)
    def _():
        y_ref[...] = jnp.zeros_like(y_ref)
        loads = slab_copies(x1_hbm, acc_ref, b, True) + slab_copies(h2_hbm, h2_ref, b, True, sem0=n_slab)
        for cp in loads:
            cp.start()
        for cp in loads:
            cp.wait()

        def fill(c, carry):
            gather_row(c, step, xin_even_ref)
            return carry

        lax.fori_loop(0, cap, fill, 0)

    @pl.when((e == n_e - 1) & (b + 1 < n_b))
    def _():
        move(h2_hbm, h2_ref, b + 1, True)

    def pipeline_step(xin_ref, xin_next_ref):
        xb = jnp.concatenate([xin_ref[k * cap_pitch:k * cap_pitch + cap, :] for k in range(n_slab)],
                             axis=1).astype(BF16)
        nxt = jnp.minimum(step + 1, last)
        for c in range(cap):
            gather_row(c, nxt, xin_next_ref)
        a = jnp.dot(xb, wg_ref[...], preferred_element_type=F32)
        u = jnp.dot(xb, wu_ref[...], preferred_element_type=F32)
        hm = (a * (1.0 / (1.0 + jnp.exp(-a))) * u).astype(BF16)
        y = jnp.dot(hm, wd_ref[...], preferred_element_type=F32)
        prev = jnp.maximum(step - 1, 0)
        live = (step > 0).astype(F32)
        for first in range(0, cap, SCATTER_BATCH):
            scatter_rows(first, prev, live)
        for k in range(n_slab):
            y_ref[k * cap_pitch:k * cap_pitch + cap, :] = y[:, k * LANES:(k + 1) * LANES]

    @pl.when(step % 2 == 0)
    def _():
        pipeline_step(xin_even_ref, xin_odd_ref)

    @pl.when(step % 2 == 1)
    def _():
        pipeline_step(xin_odd_ref, xin_even_ref)

    @pl.when((e == 0) & (b > 0))
    def _():
        stores = slab_copies(out_hbm, acc_ref, b - 1, False)
        loads = slab_copies(x1_hbm, acc_ref, b, True, sem0=n_slab)
        for cp in stores:
            cp.start()
        for st, ld in zip(stores, loads):
            st.wait()
            ld.start()
        for cp in loads:
            cp.wait()

    @pl.when(step == last)
    def _():
        def drain(j, carry):
            scatter_rows(j * SCATTER_BATCH, step, jnp.float32(1.0))
            return carry

        lax.fori_loop(0, cap // SCATTER_BATCH, drain, 0)
        move(out_hbm, acc_ref, b, False)


def _moe_call(idx_flat, gate_flat, x1, h2, wg, wu, wd, cap):
    B, S, D = x1.shape
    E, _, FF = wg.shape
    assert D % LANES == 0 and cap % SCATTER_BATCH == 0
    n_slab = D // LANES
    smem = pl.BlockSpec(memory_space=pltpu.SMEM)
    hbm = pl.BlockSpec(memory_space=pl.ANY)
    return pl.pallas_call(
        functools.partial(_moe_kernel, cap, S, D),
        grid=(B, E),
        in_specs=[smem, smem, hbm, hbm,
                  pl.BlockSpec((None, D, FF), lambda b, e: (e, 0, 0)),
                  pl.BlockSpec((None, D, FF), lambda b, e: (e, 0, 0)),
                  pl.BlockSpec((None, FF, D), lambda b, e: (e, 0, 0))],
        out_specs=hbm,
        out_shape=jax.ShapeDtypeStruct((B, S, D), F32),
        scratch_shapes=[pltpu.VMEM((n_slab * (S + SLAB_PAD), LANES), F32),
                        pltpu.VMEM((n_slab * (S + SLAB_PAD), LANES), F32),
                        pltpu.VMEM((n_slab * (cap + SLAB_PAD), LANES), F32),
                        pltpu.VMEM((n_slab * (cap + SLAB_PAD), LANES), F32),
                        pltpu.VMEM((n_slab * (cap + SLAB_PAD), LANES), F32),
                        pltpu.SemaphoreType.DMA((2 * n_slab,))],
        compiler_params=_cparams(("arbitrary", "arbitrary")),
        name="moe",
    )(idx_flat, gate_flat, x1, h2, wg, wu, wd)


def _rope_tables(S):
    half = HEAD_DIM // 2
    inv = ROPE_THETA ** (-jnp.arange(half, dtype=F32) / half)
    ang = jnp.arange(S, dtype=F32)[:, None] * inv[None, :]
    cos = jnp.cos(ang)
    sin = jnp.sin(ang)
    reps = LANE_CHUNK // HEAD_DIM
    return (jnp.tile(jnp.concatenate([cos, cos], axis=-1), (1, reps)),
            jnp.tile(jnp.concatenate([-sin, sin], axis=-1), (1, reps)))


def _layer(x, norm1_g, w_in, dil_q_g, dil_k_g, na_q_g, na_k_g, na_rpb, w_dil_branch, w_na_branch,
           w_out, norm2_g, w_router, w_gate, w_up, w_down):
    B, S, D = x.shape
    M = B * S
    tm = 512
    scale = HEAD_DIM ** -0.5
    dil_heads = DIL_WIDTH // HEAD_DIM
    gain_cols = jnp.concatenate([
        jnp.tile(dil_q_g * scale, dil_heads), jnp.tile(dil_k_g, dil_heads), jnp.ones((DIL_WIDTH,), F32),
        jnp.tile(na_q_g * scale, NA_HEADS), jnp.tile(na_k_g, NA_HEADS), jnp.ones((NA_WIDTH + 2 * D,), F32),
    ])[None, :].astype(F32)
    cos_t, sin_t = _rope_tables(S)
    blk = np.arange(LANE_CHUNK) // HEAD_DIM
    bd = jnp.asarray(blk[:, None] == blk[None, :], BF16)

    x2 = x.reshape(M, D)
    proj = _proj_call(x2, norm1_g[None, :], w_in.astype(BF16), gain_cols, cos_t, sin_t, bd, S, D, tm)
    n_groups = len(DIL_PAIRS)
    qa, ka, va = proj[:n_groups], proj[n_groups:2 * n_groups], proj[2 * n_groups:3 * n_groups]
    qn, kn, vn, ga, gn = proj[3 * n_groups:]

    o_parts, st_parts = [], []
    for g, (window, dilation) in enumerate(DIL_PAIRS):
        o, st = _dilated_call(qa[g], ka[g], va[g], B, S, g, window, dilation)
        o_parts.append(o)
        st_parts.append(st)

    yb = _na_call(qn.reshape(B, S, -1), kn.reshape(B, S, -1), vn.reshape(B, S, -1),
                  _na_bias_table(na_rpb)).reshape(M, NA_WIDTH)

    wr_t = w_router.T
    wr_hi = wr_t.astype(BF16)
    wr_lo = (wr_t - wr_hi.astype(F32)).astype(BF16)
    x1, h2, aff = _mix_call(x2, o_parts, st_parts, yb, ga, gn,
                            w_dil_branch.astype(BF16), w_na_branch.astype(BF16), w_out.astype(BF16),
                            norm2_g[None, :], wr_hi, wr_lo, B, S, MIX_TILE_ROWS)

    cap = (EC_CAPACITY_FACTOR * S) // N_EXPERTS
    idx, gates = _select_call(aff, cap)
    return _moe_call(idx.reshape(-1), gates.reshape(-1),
                     x1.reshape(B, S, D), h2.reshape(B, S, D),
                     w_gate.astype(BF16), w_up.astype(BF16), w_down.astype(BF16), cap)


def kernel(x, norm1_g, w_in, dil_q_norm_g, dil_k_norm_g, na_q_norm_g, na_k_norm_g, na_rpb, w_dil_branch,
           w_na_branch, w_out, norm2_g, w_router, w_gate, w_up, w_down):
    for l in range(norm1_g.shape[0]):
        x = _layer(x, norm1_g[l], w_in[l], dil_q_norm_g[l], dil_k_norm_g[l], na_q_norm_g[l],
                   na_k_norm_g[l], na_rpb[l], w_dil_branch[l], w_na_branch[l], w_out[l], norm2_g[l],
                   w_router[l], w_gate[l], w_up[l], w_down[l])
    return x
```

```python
import functools
import math

import jax
import jax.numpy as jnp
import numpy as np
from jax import lax
from jax.experimental import pallas as pl
from jax.experimental.pallas import tpu as pltpu

F32 = jnp.float32
BF16 = jnp.bfloat16

HEAD_DIM = 64
DIL_PAIRS = ((128, 1), (512, 4), (2048, 16))
DIL_GROUP_HEADS = 4
DIL_WIDTH = DIL_GROUP_HEADS * len(DIL_PAIRS) * HEAD_DIM
DIL_OUT = DIL_GROUP_HEADS * HEAD_DIM
NA_HEADS = 8
NA_WIDTH = NA_HEADS * HEAD_DIM
GRID_W = 64
NA_KH = 8
NA_KW = 16
N_EXPERTS = 16
EC_CAPACITY_FACTOR = 2
ROPE_THETA = 10000.0
EPS = 1e-6
MASKED = -1e30

LANE_CHUNK = 256
QUAD = 4 * HEAD_DIM
STAT_LANES = 128
LANES = 128
SLAB_PAD = 8
SCATTER_BATCH = 4
DIL_RESIDUES_PER_STEP = 4
MIX_TILE_ROWS = 1024
MIX_SUB_ROWS = 512
VMEM_LIMIT = 56 * 1024 * 1024


def _cparams(sem):
    return pltpu.CompilerParams(dimension_semantics=sem, vmem_limit_bytes=VMEM_LIMIT)


def _const_spec(shape):
    nd = len(shape)
    return pl.BlockSpec(shape, lambda *_: (0,) * nd, pipeline_mode=pl.Buffered(1))


def _proj_kernel(segs, x_ref, g1_ref, w_ref, gain_ref, cos_ref, sin_ref, bd_ref, *out_and_scratch):
    *out_refs, tmp_ref = out_and_scratch
    x = x_ref[...]
    ms = jnp.mean(x * x, axis=-1, keepdims=True)
    h = (x * lax.rsqrt(ms + EPS) * g1_ref[...]).astype(BF16)
    bd = bd_ref[...]

    def rope(y):
        partner = jnp.concatenate([y[:, LANES:], y[:, :LANES]], axis=1)
        return y * cos_ref[...] + partner * sin_ref[...]

    tm = x.shape[0]
    chunks, col = [], 0
    for ref, (width, kind, dil) in zip(out_refs, segs):
        for c in range(0, width, LANE_CHUNK):
            chunks.append((ref, c, kind, dil, col + c))
        col += width
    normed = [ch for ch in chunks if ch[2] in ("qk", "qk_rope")]
    raw = {ch[4]: jnp.dot(h, w_ref[:, ch[4]:ch[4] + LANE_CHUNK], preferred_element_type=F32) for ch in normed}
    squares = jnp.concatenate([(raw[ch[4]] * raw[ch[4]]).astype(BF16) for ch in normed], axis=0)
    sumsq = jnp.dot(squares, bd, preferred_element_type=F32)
    for i, ch in enumerate(normed):
        ss = sumsq[i * tm:(i + 1) * tm]
        raw[ch[4]] = raw[ch[4]] * lax.rsqrt(ss * (1.0 / HEAD_DIM) + EPS) * gain_ref[:, ch[4]:ch[4] + LANE_CHUNK]

    for ref, c, kind, dil, c_abs in chunks:
        if kind == "qk_rope":
            p = rope(raw[c_abs])
        elif kind == "qk":
            p = raw[c_abs]
        else:
            p = jnp.dot(h, w_ref[:, c_abs:c_abs + LANE_CHUNK], preferred_element_type=F32)
        if kind == "gate":
            p = 1.0 / (1.0 + jnp.exp(-p))
        if dil == 1:
            ref[:, c:c + LANE_CHUNK] = p.astype(ref.dtype)
        else:
            for half in range(LANE_CHUNK // LANES):
                tmp_ref[half] = p[:, half * LANES:(half + 1) * LANES]
            for r in range(dil):
                for half in range(LANE_CHUNK // LANES):
                    c0 = r * LANE_CHUNK + half * LANES
                    ref[:, c0:c0 + LANES] = tmp_ref[half, pl.ds(r, tm // dil, stride=dil), :].astype(ref.dtype)


def _proj_call(x2, g1, w_in, gain_cols, cos_t, sin_t, bd, S, d_model, tm):
    M = x2.shape[0]
    dils = [d for _, d in DIL_PAIRS]
    segs = tuple((QUAD, kind, d) for kind in ("qk_rope", "qk_rope", "v") for d in dils) + (
        (NA_WIDTH, "qk", 1), (NA_WIDTH, "qk", 1), (NA_WIDTH, "v", 1),
        (d_model, "gate", 1), (d_model, "gate", 1))
    n_cols = sum(w for w, _, _ in segs)
    pos_blocks = S // tm
    row = lambda i: (i, 0)

    return pl.pallas_call(
        functools.partial(_proj_kernel, segs),
        grid=(M // tm,),
        in_specs=[
            pl.BlockSpec((tm, d_model), row),
            _const_spec((1, d_model)),
            _const_spec((d_model, n_cols)),
            _const_spec((1, n_cols)),
            pl.BlockSpec((tm, LANE_CHUNK), lambda i: (i % pos_blocks, 0)),
            pl.BlockSpec((tm, LANE_CHUNK), lambda i: (i % pos_blocks, 0)),
            _const_spec((LANE_CHUNK, LANE_CHUNK)),
        ],
        out_specs=[pl.BlockSpec((tm // d, w * d), row) for w, _, d in segs],
        out_shape=[jax.ShapeDtypeStruct((M // d, w * d), BF16) for w, _, d in segs],
        scratch_shapes=[pltpu.VMEM((LANE_CHUNK // LANES, tm, LANES), F32)],
        compiler_params=_cparams(("parallel",)),
        name="proj",
    )(x2, g1, w_in, gain_cols, cos_t, sin_t, bd)


def _split_half_order():
    n = np.arange(QUAD)
    half, head, i = n // LANES, (n % LANES) // (HEAD_DIM // 2), n % (HEAD_DIM // 2)
    return head * HEAD_DIM + half * (HEAD_DIM // 2) + i


def _stack_heads(q):
    lane = lax.broadcasted_iota(jnp.int32, (1, QUAD), 1)
    q_head = (lane % LANES) // (HEAD_DIM // 2)
    zero = jnp.zeros_like(q)
    return jnp.concatenate([jnp.where(q_head == h, q, zero) for h in range(4)], axis=0)


def _pick_heads(parts, lane_head):
    out = parts[3]
    for h in (2, 1, 0):
        out = jnp.where(lane_head == h, parts[h], out)
    return out


def _dil_kernel(L, T, R, n_res, q_ref, k_ref, v_ref, o_ref, st_ref):
    KW = T + 2 * R
    lane_head = lax.broadcasted_iota(jnp.int32, (1, QUAD), 1) // HEAD_DIM
    stat_lane = lax.broadcasted_iota(jnp.int32, (1, STAT_LANES), 1)
    qi = lax.broadcasted_iota(jnp.int32, (T, KW), 0)
    kj = lax.broadcasted_iota(jnp.int32, (T, KW), 1)

    def body(i, carry):
        q0 = pl.multiple_of(i * T, T)
        ks = pl.multiple_of(jnp.clip(q0 - R, 0, L - KW), R)
        off = kj - qi + (ks - q0)
        bias = jnp.where(jnp.abs(off) <= R, 0.0, MASKED).astype(F32)
        bias = jnp.concatenate([bias] * 4, axis=0)
        for res in range(n_res):
            block(q0, ks, bias, slice(res * QUAD, (res + 1) * QUAD),
                  slice(res * STAT_LANES, (res + 1) * STAT_LANES))
        return carry

    def block(q0, ks, bias, lanes, stat_lanes):
        q = q_ref[pl.ds(q0, T), lanes]
        k = k_ref[pl.ds(ks, KW), lanes]
        v = v_ref[pl.ds(ks, KW), lanes]
        s = lax.dot_general(_stack_heads(q), k,(((1,), (1,)), ((), ())),
                            preferred_element_type=F32)
        s = s + bias
        m = jnp.max(s, axis=-1, keepdims=True)
        p = jnp.exp(s - m)
        den = jnp.sum(p, axis=-1, keepdims=True)
        o = jnp.dot(p.astype(BF16), v, preferred_element_type=F32) / den
        stat = jnp.zeros((T, STAT_LANES), F32)
        for h in range(4):
            stat = jnp.where(stat_lane == h, m[h * T:(h + 1) * T], stat)
            stat = jnp.where(stat_lane == 4 + h, den[h * T:(h + 1) * T], stat)
        outs = [o[h * T:(h + 1) * T] for h in range(4)]
        o_ref[pl.ds(q0, T), lanes] = _pick_heads(outs, lane_head).astype(o_ref.dtype)
        st_ref[pl.ds(q0, T), stat_lanes] = stat

    lax.fori_loop(0, L // T, body, 0, unroll=max(1, 4 // n_res))


def _dilated_call(q, k, v, B, S, group, window, dilation, T=128):
    d = dilation
    R = (window // 2) // d
    L = S // d
    assert L % T == 0 and L >= T + 2 * R and R % 16 == 0
    view = lambda t: t.reshape(B, L, d * QUAD)
    n_res = min(d, DIL_RESIDUES_PER_STEP)
    in_spec = pl.BlockSpec((None, L, n_res * QUAD), lambda b, r: (b, 0, r))
    o, st = pl.pallas_call(
        functools.partial(_dil_kernel, L, T, R, n_res),
        grid=(B, d // n_res),
        in_specs=[in_spec, in_spec, in_spec],
        out_specs=[pl.BlockSpec((None, L, n_res * QUAD), lambda b, r: (b, 0, r)),
                   pl.BlockSpec((None, L, n_res * STAT_LANES), lambda b, r: (b, 0, r))],
        out_shape=[jax.ShapeDtypeStruct((B, L, d * QUAD), BF16),
                   jax.ShapeDtypeStruct((B, L, d * STAT_LANES), F32)],
        compiler_params=_cparams(("parallel", "parallel")),
        name=f"dilated_g{group}",
    )(view(q), view(k), view(v))
    return o.reshape(B * L, d * QUAD), st.reshape(B * L, d * STAT_LANES)


def _na_kernel(rows, RB, q_ref, k_ref, v_ref, bias_ref, o_ref):
    lane_head = lax.broadcasted_iota(jnp.int32, (1, QUAD), 1) // HEAD_DIM
    n_keys = NA_KH * GRID_W
    i = pl.program_id(1)
    for rr in range(RB):
        r = i * RB + rr
        rs = jnp.clip(r - NA_KH // 2, 0, rows - NA_KH)
        e = rs - r + NA_KH - 1
        k0 = pl.multiple_of(rs * GRID_W, GRID_W)
        for quad in range(NA_WIDTH // QUAD):
            lanes = slice(quad * QUAD, (quad + 1) * QUAD)
            q = q_ref[rr * GRID_W:(rr + 1) * GRID_W, lanes]
            k = k_ref[pl.ds(k0, n_keys), lanes]
            v = v_ref[pl.ds(k0, n_keys), lanes]
            s = lax.dot_general(_stack_heads(q), k,(((1,), (1,)), ((), ())),
                                preferred_element_type=F32)
            s = s + bias_ref[quad, e]
            m = jnp.max(s, axis=-1, keepdims=True)
            p = jnp.exp(s - m)
            den = jnp.sum(p, axis=-1, keepdims=True)
            o = jnp.dot(p.astype(BF16), v, preferred_element_type=F32) / den
            outs = [o[h * GRID_W:(h + 1) * GRID_W] for h in range(4)]
            o_ref[rr * GRID_W:(rr + 1) * GRID_W, lanes] = _pick_heads(outs, lane_head).astype(o_ref.dtype)


def _na_bias_table(rpb):
    c = np.arange(GRID_W)
    ws = np.clip(c - NA_KW // 2, 0, GRID_W - NA_KW)
    col_ok = (c[None, :] >= ws[:, None]) & (c[None, :] < ws[:, None] + NA_KW)
    dc = np.clip(c[None, :] - c[:, None] + NA_KW - 1, 0, 2 * NA_KW - 2)
    n_dc = 2 * NA_KW - 1
    n_heads, n_dr = rpb.shape[0], rpb.shape[1]
    onehot = np.zeros((n_dc, GRID_W * GRID_W), np.float32)
    onehot[dc.reshape(-1), np.arange(GRID_W * GRID_W)] = 1.0
    a = jnp.dot(rpb.reshape(n_heads * n_dr, n_dc).astype(F32), onehot, precision=lax.Precision.HIGHEST)
    a = jnp.where(col_ok[None, None], a.reshape(n_heads, n_dr, GRID_W, GRID_W), MASKED)
    t = jnp.stack([a[:, e:e + NA_KH] for e in range(NA_KH)], axis=1)
    t = t.reshape(n_heads // 4, 4, NA_KH, NA_KH, GRID_W, GRID_W).transpose(0, 2, 1, 4, 3, 5)
    return t.reshape(n_heads // 4, NA_KH, 4 * GRID_W, NA_KH * GRID_W)


def _na_call(qn, kn, vn, bias, RB=8):
    B, S, _ = qn.shape
    rows = S // GRID_W
    assert rows >= NA_KH and rows % RB == 0
    blk = pl.BlockSpec((None, RB * GRID_W, NA_WIDTH), lambda b, i: (b, i, 0))
    full = pl.BlockSpec((None, S, NA_WIDTH), lambda b, i: (b, 0, 0))
    return pl.pallas_call(
        functools.partial(_na_kernel, rows, RB),
        grid=(B, rows // RB),
        in_specs=[blk, full, full, _const_spec(bias.shape)],
        out_specs=blk,
        out_shape=jax.ShapeDtypeStruct((B, S, NA_WIDTH), BF16),
        compiler_params=_cparams(("parallel", "arbitrary")),
        name="neighbourhood",
    )(qn, kn, vn, bias)


def _token_major(o_ref, s_ref, o_tmp, s_tmp, dil):
    if dil == 1:
        return o_ref[...].astype(F32), s_ref[...]
    rows = o_ref.shape[0]
    halves = QUAD // LANES
    for r in range(dil):
        blk = o_ref[:, r * QUAD:(r + 1) * QUAD].astype(F32)
        for half in range(halves):
            o_tmp[half, pl.ds(r, rows, stride=dil), :] = blk[:, half * LANES:(half + 1) * LANES]
        s_tmp[pl.ds(r, rows, stride=dil), :] = s_ref[:, r * STAT_LANES:(r + 1) * STAT_LANES]
    return jnp.concatenate([o_tmp[half] for half in range(halves)], axis=1), s_tmp[...]


def _mix_kernel(dils, x_ref, o0_ref, o1_ref, o2_ref, s0_ref, s1_ref, s2_ref, yb_ref, ga_ref, gn_ref,
                pa_ref, pb_ref, wo_ref, g2_ref, wrh_ref, wrl_ref, x1_ref, h2_ref, aff_ref, o_tmp, s_tmp):
    tile_outs, tile_stats = [], []
    for g, (o_ref, s_ref, dil) in enumerate(zip((o0_ref, o1_ref, o2_ref), (s0_ref, s1_ref, s2_ref), dils)):
        o, st = _token_major(o_ref, s_ref, o_tmp.at[g], s_tmp.at[g], dil)
        tile_outs.append(o)
        tile_stats.append(st)
    for r0 in range(0, x_ref.shape[0], MIX_SUB_ROWS):
        rows = slice(r0, r0 + MIX_SUB_ROWS)
        _mix_rows([o[rows] for o in tile_outs], [st[rows] for st in tile_stats],
                  x_ref.at[rows], yb_ref.at[rows], ga_ref.at[rows], gn_ref.at[rows],
                  pa_ref, pb_ref, wo_ref, g2_ref, wrh_ref, wrl_ref,
                  x1_ref.at[rows], h2_ref.at[rows], aff_ref.at[:, rows])


def _mix_rows(outs, stats, x_ref, yb_ref, ga_ref, gn_ref, pa_ref, pb_ref, wo_ref, g2_ref, wrh_ref, wrl_ref,
              x1_ref, h2_ref, aff_ref):
    lane_head = lax.broadcasted_iota(jnp.int32, (1, QUAD), 1) // HEAD_DIM
    coef = [None, None, None]
    for h in range(4):
        ms = [st[:, h:h + 1] for st in stats]
        dens = [st[:, 4 + h:5 + h] for st in stats]
        top = jnp.maximum(jnp.maximum(ms[0], ms[1]), ms[2])
        ws = [dens[g] * jnp.exp(ms[g] - top) for g in range(3)]
        tot = ws[0] + ws[1] + ws[2]
        for g in range(3):
            cg = ws[g] / tot
            coef[g] = cg if coef[g] is None else jnp.where(lane_head == h, cg, coef[g])
    ya = coef[0] * outs[0] + coef[1] * outs[1] + coef[2] * outs[2]
    ta = jnp.dot(ya.astype(BF16), pa_ref[...], preferred_element_type=F32)
    tb = jnp.dot(yb_ref[...], pb_ref[...], preferred_element_type=F32)
    merged = ga_ref[...].astype(F32) * ta + gn_ref[...].astype(F32) * tb
    x1 = x_ref[...] + jnp.dot(merged.astype(BF16), wo_ref[...], preferred_element_type=F32)
    x1_ref[...] = x1
    ms2 = jnp.mean(x1 * x1, axis=-1, keepdims=True)
    h2 = x1 * lax.rsqrt(ms2 + EPS) * g2_ref[...]
    h2_ref[...] = h2
    h_hi = h2.astype(BF16)
    h_lo = (h2 - h_hi.astype(F32)).astype(BF16)
    nt = (((1,), (1,)), ((), ()))
    both = lax.dot_general(jnp.concatenate([wrh_ref[...], wrl_ref[...]], axis=0), h_hi, nt,
                           preferred_element_type=F32)
    n_exp = wrh_ref.shape[0]
    logits = (both[:n_exp] + lax.dot_general(wrh_ref[...], h_lo, nt, preferred_element_type=F32)
              + both[n_exp:])
    z = jnp.exp(logits - jnp.max(logits, axis=0, keepdims=True))
    aff_ref[...] = z / jnp.sum(z, axis=0, keepdims=True)


def _mix_call(x2, o_parts, st_parts, yb, ga, gn, pa, pb, wo, g2, wr_hi, wr_lo, B, S, tm):
    M, D = x2.shape
    row = lambda i: (i, 0)
    blocks = S // tm
    rows_of = lambda w, d=1: pl.BlockSpec((tm // d, w * d), row)
    dils = tuple(d for _, d in DIL_PAIRS)
    return pl.pallas_call(
        functools.partial(_mix_kernel, dils),
        grid=(M // tm,),
        in_specs=[rows_of(D)] + [rows_of(QUAD, d) for d in dils] + [rows_of(STAT_LANES, d) for d in dils]
                 + [rows_of(NA_WIDTH), rows_of(D), rows_of(D),
                    _const_spec(pa.shape), _const_spec(pb.shape), _const_spec(wo.shape),
                    _const_spec(g2.shape), _const_spec(wr_hi.shape), _const_spec(wr_lo.shape)],
        out_specs=[rows_of(D), rows_of(D),
                   pl.BlockSpec((None, N_EXPERTS, tm), lambda i: (i // blocks, 0, i % blocks))],
        out_shape=[jax.ShapeDtypeStruct((M, D), F32), jax.ShapeDtypeStruct((M, D), F32),
                   jax.ShapeDtypeStruct((B, N_EXPERTS, S), F32)],
        scratch_shapes=[pltpu.VMEM((len(dils), QUAD // LANES, tm, LANES), F32),
                        pltpu.VMEM((len(dils), tm, LANES), F32)],
        compiler_params=_cparams(("parallel",)),
        name="mix",
    )(x2, *o_parts, *st_parts, yb, ga, gn, pa, pb, wo, g2, wr_hi, wr_lo)


def _select_kernel(cap, n_exp, rows, aff_ref, tri_ref, below_ref, idx_ref, gate_ref):
    n_bits = 31
    affs = [aff_ref[e * rows:(e + 1) * rows, :] for e in range(n_exp)]
    as_float = lambda word: pltpu.bitcast(word, F32)

    def count(mask):
        return jnp.sum(jnp.sum(mask.astype(F32), axis=1, keepdims=True), axis=0, keepdims=True)

    def refine(i, thrs):
        bit = lax.shift_left(jnp.int32(1), n_bits - 1 - i)
        out = []
        for e in range(n_exp):
            cand = thrs[e] | bit
            out.append(jnp.where(count(affs[e] >= as_float(cand)) >= cap, cand, thrs[e]))
        return tuple(out)

    thrs = lax.fori_loop(0, n_bits, refine, tuple(jnp.zeros((1, 1), jnp.int32) for _ in range(n_exp)))

    tri = tri_ref[...]
    below = below_ref[...]

    def running_count(mask):
        in_row = jnp.dot(mask.astype(BF16), tri, preferred_element_type=F32)
        row_tot = jnp.broadcast_to(in_row[:, LANES - 1:LANES], in_row.shape).astype(BF16)
        return in_row + jnp.dot(below, row_tot, preferred_element_type=F32)

    cuts = [as_float(t) for t in thrs]
    above = jnp.concatenate([affs[e] > cuts[e] for e in range(n_exp)], axis=0)
    equal = jnp.concatenate([affs[e] == cuts[e] for e in range(n_exp)], axis=0)
    room = jnp.concatenate(
        [jnp.broadcast_to(cap - count(affs[e] > cuts[e]), (rows, LANES)) for e in range(n_exp)], axis=0)
    chosen = above | (equal & (running_count(equal) <= room))
    cnt = running_count(chosen)

    lane = lax.broadcasted_iota(jnp.int32, (1, LANES), 1)
    lane_f = lane.astype(F32)
    slot = lax.broadcasted_iota(jnp.int32, (cap, 1), 0).astype(F32)
    diag = lax.broadcasted_iota(jnp.int32, (rows, LANES), 0) == lax.broadcasted_iota(jnp.int32, (rows, LANES), 1)
    pad = jnp.zeros((LANES - rows, LANES), BF16)
    idx_out = jnp.zeros((cap, LANES), jnp.int32)
    gate_out = jnp.zeros((cap, LANES), F32)
    for e in range(n_exp):
        cnt_e = cnt[e * rows:(e + 1) * rows]
        aff_e = aff_ref[e * rows:(e + 1) * rows, :]
        row_end = jnp.sum(jnp.where(diag, cnt_e[:, LANES - 1:LANES], 0.0), axis=0, keepdims=True)
        row_end = jnp.where(lane < rows, row_end, float(2 * cap))
        row_of = jnp.sum((row_end <= slot).astype(F32), axis=1, keepdims=True)
        pick = (lane_f == row_of).astype(BF16)
        cnt_hi = jnp.floor(cnt_e * (1.0 / 256.0))
        a_hi = aff_e.astype(BF16)
        a_r1 = aff_e - a_hi.astype(F32)
        a_mid = a_r1.astype(BF16)
        a_lo = (a_r1 - a_mid.astype(F32)).astype(BF16)
        pieces = [cnt_hi.astype(BF16), (cnt_e - 256.0 * cnt_hi).astype(BF16), a_hi, a_mid, a_lo]
        rhs = jnp.concatenate([jnp.concatenate([p, pad], axis=0) for p in pieces], axis=1)
        got = jnp.dot(pick, rhs, preferred_element_type=F32)
        row_cnt = 256.0 * got[:, 0:LANES] + got[:, LANES:2 * LANES]
        row_aff = got[:, 2 * LANES:3 * LANES] + got[:, 3 * LANES:4 * LANES] + got[:, 4 * LANES:5 * LANES]
        lane_of = jnp.sum((row_cnt <= slot).astype(F32), axis=1, keepdims=True)
        gate = jnp.sum(jnp.where(lane_f == lane_of, row_aff, 0.0), axis=1, keepdims=True)
        token = (row_of * float(LANES) + lane_of).astype(jnp.int32)
        idx_out = jnp.where(lane == e, token, idx_out)
        gate_out = jnp.where(lane == e, gate, gate_out)
    idx_ref[...] = idx_out
    gate_ref[...] = gate_out


def _select_call(aff, cap):
    B, E, S = aff.shape
    rows = S // LANES
    assert S % LANES == 0 and rows <= LANES and rows % 8 == 0 and cap <= 256 * 256
    r = np.arange(E * rows)
    tri = jnp.asarray(np.arange(LANES)[:, None] <= np.arange(LANES)[None, :], BF16)
    below = jnp.asarray((r[:, None] // rows == r[None, :] // rows) & (r[None, :] < r[:, None]), BF16)
    out_spec = pl.BlockSpec((None, cap, LANES), lambda b: (b, 0, 0))
    idx, gate = pl.pallas_call(
        functools.partial(_select_kernel, cap, E, rows),
        grid=(B,),
        in_specs=[pl.BlockSpec((None, E * rows, LANES), lambda b: (b, 0, 0)),
                  _const_spec(tri.shape), _const_spec(below.shape)],
        out_specs=[out_spec, out_spec],
        out_shape=[jax.ShapeDtypeStruct((B, cap, LANES), jnp.int32),
                   jax.ShapeDtypeStruct((B, cap, LANES), F32)],
        compiler_params=_cparams(("parallel",)),
        name="select",
    )(aff.reshape(B, E * rows, LANES), tri, below)
    unpack = lambda t: t[:, :, :E].transpose(0, 2, 1)
    return unpack(idx), unpack(gate)


def _moe_kernel(cap, S, D, idx_ref, gate_ref, x1_hbm, h2_hbm, wg_ref, wu_ref, wd_ref, out_hbm,
                acc_ref, h2_ref, xin_even_ref, xin_odd_ref, y_ref, slab_sem):
    n_slab = D // LANES
    seq_pitch = S + SLAB_PAD
    cap_pitch = cap + SLAB_PAD
    b = pl.program_id(0)
    e = pl.program_id(1)
    n_b = pl.num_programs(0)
    n_e = pl.num_programs(1)
    step = b * n_e + e
    last = n_b * n_e - 1

    def slab_copies(hbm, vmem_ref, seq, load, sem0=0):
        copies = []
        for k in range(n_slab):
            vm = vmem_ref.at[pl.ds(k * seq_pitch, S), :]
            hb = hbm.at[seq, :, pl.ds(k * LANES, LANES)]
            sem = slab_sem.at[sem0 + k]
            copies.append(pltpu.make_async_copy(hb, vm, sem) if load else pltpu.make_async_copy(vm, hb, sem))
        return copies

    def move(hbm, vmem_ref, seq, load):
        copies = slab_copies(hbm, vmem_ref, seq, load)
        for cp in copies:
            cp.start()
        for cp in copies:
            cp.wait()

    def gather_row(c, src_step, xin_ref):
        t = idx_ref[src_step * cap + c]
        xin_ref[pl.ds(c, n_slab, stride=cap_pitch), :] = h2_ref[pl.ds(t, n_slab, stride=seq_pitch), :]

    def scatter_rows(first, src_step, live):
        updates = []
        for i in range(SCATTER_BATCH):
            t = idx_ref[src_step * cap + first + i]
            g = gate_ref[src_step * cap + first + i] * live
            w = y_ref[pl.ds(first + i, n_slab, stride=cap_pitch), :] * g
            updates.append((t, acc_ref[pl.ds(t, n_slab, stride=seq_pitch), :] + w))
        for t, v in updates:
            acc_ref[pl.ds(t, n_slab, stride=seq_pitch), :] = v

    @pl.when(step == 0)
    def _():
        y_ref[...] = jnp.zeros_like(y_ref)
        loads = slab_copies(x1_hbm, acc_ref, b, True) + slab_copies(h2_hbm, h2_ref, b, True, sem0=n_slab)
        for cp in loads:
            cp.start()
        for cp in loads:
            cp.wait()

        def fill(c, carry):
            gather_row(c, step, xin_even_ref)
            return carry

        lax.fori_loop(0, cap, fill, 0)

    @pl.when((e == n_e - 1) & (b + 1 < n_b))
    def _():
        move(h2_hbm, h2_ref, b + 1, True)

    def pipeline_step(xin_ref, xin_next_ref):
        xb = jnp.concatenate([xin_ref[k * cap_pitch:k * cap_pitch + cap, :] for k in range(n_slab)],
                             axis=1).astype(BF16)
        nxt = jnp.minimum(step + 1, last)
        for c in range(cap):
            gather_row(c, nxt, xin_next_ref)
        a = jnp.dot(xb, wg_ref[...], preferred_element_type=F32)
        u = jnp.dot(xb, wu_ref[...], preferred_element_type=F32)
        hm = (a * (1.0 / (1.0 + jnp.exp(-a))) * u).astype(BF16)
        y = jnp.dot(hm, wd_ref[...], preferred_element_type=F32)
        prev = jnp.maximum(step - 1, 0)
        live = (step > 0).astype(F32)
        for first in range(0, cap, SCATTER_BATCH):
            scatter_rows(first, prev, live)
        for k in range(n_slab):
            y_ref[k * cap_pitch:k * cap_pitch + cap, :] = y[:, k * LANES:(k + 1) * LANES]

    @pl.when(step % 2 == 0)
    def _():
        pipeline_step(xin_even_ref, xin_odd_ref)

    @pl.when(step % 2 == 1)
    def _():
        pipeline_step(xin_odd_ref, xin_even_ref)

    @pl.when((e == 0) & (b > 0))
    def _():
        stores = slab_copies(out_hbm, acc_ref, b - 1, False)
        loads = slab_copies(x1_hbm, acc_ref, b, True, sem0=n_slab)
        for cp in stores:
            cp.start()
        for st, ld in zip(stores, loads):
            st.wait()
            ld.start()
        for cp in loads:
            cp.wait()

    @pl.when(step == last)
    def _():
        def drain(j, carry):
            scatter_rows(j * SCATTER_BATCH, step, jnp.float32(1.0))
            return carry

        lax.fori_loop(0, cap // SCATTER_BATCH, drain, 0)
        move(out_hbm, acc_ref, b, False)


def _moe_call(idx_flat, gate_flat, x1, h2, wg, wu, wd, cap):
    B, S, D = x1.shape
    E, _, FF = wg.shape
    assert D % LANES == 0 and cap % SCATTER_BATCH == 0
    n_slab = D // LANES
    smem = pl.BlockSpec(memory_space=pltpu.SMEM)
    hbm = pl.BlockSpec(memory_space=pl.ANY)
    return pl.pallas_call(
        functools.partial(_moe_kernel, cap, S, D),
        grid=(B, E),
        in_specs=[smem, smem, hbm, hbm,
                  pl.BlockSpec((None, D, FF), lambda b, e: (e, 0, 0)),
                  pl.BlockSpec((None, D, FF), lambda b, e: (e, 0, 0)),
                  pl.BlockSpec((None, FF, D), lambda b, e: (e, 0, 0))],
        out_specs=hbm,
        out_shape=jax.ShapeDtypeStruct((B, S, D), F32),
        scratch_shapes=[pltpu.VMEM((n_slab * (S + SLAB_PAD), LANES), F32),
                        pltpu.VMEM((n_slab * (S + SLAB_PAD), LANES), F32),
                        pltpu.VMEM((n_slab * (cap + SLAB_PAD), LANES), F32),
                        pltpu.VMEM((n_slab * (cap + SLAB_PAD), LANES), F32),
                        pltpu.VMEM((n_slab * (cap + SLAB_PAD), LANES), F32),
                        pltpu.SemaphoreType.DMA((2 * n_slab,))],
        compiler_params=_cparams(("arbitrary", "arbitrary")),
        name="moe",
    )(idx_flat, gate_flat, x1, h2, wg, wu, wd)


def _rope_tables(S):
    half = HEAD_DIM // 2
    inv = ROPE_THETA ** (-jnp.arange(half, dtype=F32) / half)
    ang = jnp.arange(S, dtype=F32)[:, None] * inv[None, :]
    cos = jnp.cos(ang)
    sin = jnp.sin(ang)
    heads = LANE_CHUNK // HEAD_DIM
    return (jnp.tile(cos, (1, 2 * heads)),
            jnp.concatenate([jnp.tile(-sin, (1, heads)), jnp.tile(sin, (1, heads))], axis=-1))


def _split_half_columns(w, start, width):
    rows = w.shape[0]
    seg = w[:, start:start + width].reshape(rows, width // QUAD, 4, 2, HEAD_DIM // 2)
    seg = seg.transpose(0, 1, 3, 2, 4).reshape(rows, width)
    return jnp.concatenate([w[:, :start], seg, w[:, start + width:]], axis=1)


def _layer(x, norm1_g, w_in, dil_q_g, dil_k_g, na_q_g, na_k_g, na_rpb, w_dil_branch, w_na_branch,
           w_out, norm2_g, w_router, w_gate, w_up, w_down):
    B, S, D = x.shape
    M = B * S
    tm = 512
    scale = HEAD_DIM ** -0.5
    dil_heads = DIL_WIDTH // HEAD_DIM
    half = HEAD_DIM // 2

    def quad_gains(g, n_heads):
        quad = jnp.concatenate([jnp.tile(g[:half], 4), jnp.tile(g[half:], 4)])
        return jnp.tile(quad, n_heads // 4)

    gain_cols = jnp.concatenate([
        quad_gains(dil_q_g * scale, dil_heads), quad_gains(dil_k_g, dil_heads), jnp.ones((DIL_WIDTH,), F32),
        quad_gains(na_q_g * scale, NA_HEADS), quad_gains(na_k_g, NA_HEADS), jnp.ones((NA_WIDTH + 2 * D,), F32),
    ])[None, :].astype(F32)
    cos_t, sin_t = _rope_tables(S)
    head_of_lane = _split_half_order() // HEAD_DIM
    bd = jnp.asarray(head_of_lane[:, None] == head_of_lane[None, :], BF16)
    w_cols = _split_half_columns(w_in, 0, 2 * DIL_WIDTH)
    w_cols = _split_half_columns(w_cols, 3 * DIL_WIDTH, 2 * NA_WIDTH)

    x2 = x.reshape(M, D)
    proj = _proj_call(x2, norm1_g[None, :], w_cols.astype(BF16), gain_cols, cos_t, sin_t, bd, S, D, tm)
    n_groups = len(DIL_PAIRS)
    qa, ka, va = proj[:n_groups], proj[n_groups:2 * n_groups], proj[2 * n_groups:3 * n_groups]
    qn, kn, vn, ga, gn = proj[3 * n_groups:]

    o_parts, st_parts = [], []
    for g, (window, dilation) in enumerate(DIL_PAIRS):
        o, st = _dilated_call(qa[g], ka[g], va[g], B, S, g, window, dilation)
        o_parts.append(o)
        st_parts.append(st)

    yb = _na_call(qn.reshape(B, S, -1), kn.reshape(B, S, -1), vn.reshape(B, S, -1),
                  _na_bias_table(na_rpb)).reshape(M, NA_WIDTH)

    wr_t = w_router.T
    wr_hi = wr_t.astype(BF16)
    wr_lo = (wr_t - wr_hi.astype(F32)).astype(BF16)
    x1, h2, aff = _mix_call(x2, o_parts, st_parts, yb, ga, gn,
                            w_dil_branch.astype(BF16), w_na_branch.astype(BF16), w_out.astype(BF16),
                            norm2_g[None, :], wr_hi, wr_lo, B, S, MIX_TILE_ROWS)

    cap = (EC_CAPACITY_FACTOR * S) // N_EXPERTS
    idx, gates = _select_call(aff, cap)
    return _moe_call(idx.reshape(-1), gates.reshape(-1),
                     x1.reshape(B, S, D), h2.reshape(B, S, D),
                     w_gate.astype(BF16), w_up.astype(BF16), w_down.astype(BF16), cap)


def kernel(x, norm1_g, w_in, dil_q_norm_g, dil_k_norm_g, na_q_norm_g, na_k_norm_g, na_rpb, w_dil_branch,
           w_na_branch, w_out, norm2_g, w_router, w_gate, w_up, w_down):
    for l in range(norm1_g.shape[0]):
        x = _layer(x, norm1_g[l], w_in[l], dil_q_norm_g[l], dil_k_norm_g[l], na_q_norm_g[l],
                   na_k_norm_g[l], na_rpb[l], w_dil_branch[l], w_na_branch[l], w_out[l], norm2_g[l],
                   w_router[l], w_gate[l], w_up[l], w_down[l])
    return x
```

```python
import functools
import math

import jax
import jax.numpy as jnp
import numpy as np
from jax import lax
from jax.experimental import pallas as pl
from jax.experimental.pallas import tpu as pltpu

F32 = jnp.float32
BF16 = jnp.bfloat16

HEAD_DIM = 64
DIL_PAIRS = ((128, 1), (512, 4), (2048, 16))
DIL_GROUP_HEADS = 4
DIL_WIDTH = DIL_GROUP_HEADS * len(DIL_PAIRS) * HEAD_DIM
DIL_OUT = DIL_GROUP_HEADS * HEAD_DIM
NA_HEADS = 8
NA_WIDTH = NA_HEADS * HEAD_DIM
GRID_W = 64
NA_KH = 8
NA_KW = 16
N_EXPERTS = 16
EC_CAPACITY_FACTOR = 2
ROPE_THETA = 10000.0
EPS = 1e-6
MASKED = -1e30

LANE_CHUNK = 256
QUAD = 4 * HEAD_DIM
STAT_LANES = 128
LANES = 128
SLAB_PAD = 8
SCATTER_BATCH = 4
DIL_RESIDUES_PER_STEP = 4
MIX_TILE_ROWS = 1024
MIX_SUB_ROWS = 512
VMEM_LIMIT = 56 * 1024 * 1024


def _cparams(sem):
    return pltpu.CompilerParams(dimension_semantics=sem, vmem_limit_bytes=VMEM_LIMIT)


def _const_spec(shape):
    nd = len(shape)
    return pl.BlockSpec(shape, lambda *_: (0,) * nd, pipeline_mode=pl.Buffered(1))


def _proj_kernel(segs, x_ref, g1_ref, w_ref, gain_ref, cos_ref, sin_ref, bd_ref, *out_and_scratch):
    *out_refs, tmp_ref = out_and_scratch
    x = x_ref[...]
    ms = jnp.mean(x * x, axis=-1, keepdims=True)
    h = (x * lax.rsqrt(ms + EPS) * g1_ref[...]).astype(BF16)
    bd = bd_ref[...]

    def rope(y):
        partner = jnp.concatenate([y[:, LANES:], y[:, :LANES]], axis=1)
        return y * cos_ref[...] + partner * sin_ref[...]

    tm = x.shape[0]
    chunks, col = [], 0
    for ref, (width, kind, dil) in zip(out_refs, segs):
        for c in range(0, width, LANE_CHUNK):
            chunks.append((ref, c, kind, dil, col + c))
        col += width
    raw = {}
    for norm_kind in ("qk_rope", "qk"):
        normed = [ch for ch in chunks if ch[2] == norm_kind]
        for ch in normed:
            raw[ch[4]] = jnp.dot(h, w_ref[:, ch[4]:ch[4] + LANE_CHUNK], preferred_element_type=F32)
        squares = jnp.concatenate([(raw[ch[4]] * raw[ch[4]]).astype(BF16) for ch in normed], axis=0)
        sumsq = jnp.dot(squares, bd, preferred_element_type=F32)
        for i, ch in enumerate(normed):
            ss = sumsq[i * tm:(i + 1) * tm]
            raw[ch[4]] = (raw[ch[4]] * lax.rsqrt(ss * (1.0 / HEAD_DIM) + EPS)
                          * gain_ref[:, ch[4]:ch[4] + LANE_CHUNK])

    for ref, c, kind, dil, c_abs in chunks:
        if kind == "qk_rope":
            p = rope(raw[c_abs])
        elif kind == "qk":
            p = raw[c_abs]
        else:
            p = jnp.dot(h, w_ref[:, c_abs:c_abs + LANE_CHUNK], preferred_element_type=F32)
        if kind == "gate":
            p = 1.0 / (1.0 + jnp.exp(-p))
        if dil == 1:
            ref[:, c:c + LANE_CHUNK] = p.astype(ref.dtype)
        else:
            for half in range(LANE_CHUNK // LANES):
                tmp_ref[half] = p[:, half * LANES:(half + 1) * LANES]
            for r in range(dil):
                for half in range(LANE_CHUNK // LANES):
                    c0 = r * LANE_CHUNK + half * LANES
                    ref[:, c0:c0 + LANES] = tmp_ref[half, pl.ds(r, tm // dil, stride=dil), :].astype(ref.dtype)


def _proj_call(x2, g1, w_in, gain_cols, cos_t, sin_t, bd, S, d_model, tm):
    M = x2.shape[0]
    dils = [d for _, d in DIL_PAIRS]
    segs = tuple((QUAD, kind, d) for kind in ("qk_rope", "qk_rope", "v") for d in dils) + (
        (NA_WIDTH, "qk", 1), (NA_WIDTH, "qk", 1), (NA_WIDTH, "v", 1),
        (d_model, "gate", 1), (d_model, "gate", 1))
    n_cols = sum(w for w, _, _ in segs)
    pos_blocks = S // tm
    row = lambda i: (i, 0)

    return pl.pallas_call(
        functools.partial(_proj_kernel, segs),
        grid=(M // tm,),
        in_specs=[
            pl.BlockSpec((tm, d_model), row),
            _const_spec((1, d_model)),
            _const_spec((d_model, n_cols)),
            _const_spec((1, n_cols)),
            pl.BlockSpec((tm, LANE_CHUNK), lambda i: (i % pos_blocks, 0)),
            pl.BlockSpec((tm, LANE_CHUNK), lambda i: (i % pos_blocks, 0)),
            _const_spec((LANE_CHUNK, LANE_CHUNK)),
        ],
        out_specs=[pl.BlockSpec((tm // d, w * d), row) for w, _, d in segs],
        out_shape=[jax.ShapeDtypeStruct((M // d, w * d), BF16) for w, _, d in segs],
        scratch_shapes=[pltpu.VMEM((LANE_CHUNK // LANES, tm, LANES), F32)],
        compiler_params=_cparams(("parallel",)),
        name="proj",
    )(x2, g1, w_in, gain_cols, cos_t, sin_t, bd)


def _split_half_order():
    n = np.arange(QUAD)
    half, head, i = n // LANES, (n % LANES) // (HEAD_DIM // 2), n % (HEAD_DIM // 2)
    return head * HEAD_DIM + half * (HEAD_DIM // 2) + i


def _stack_heads(q):
    lane = lax.broadcasted_iota(jnp.int32, (1, QUAD), 1)
    q_head = (lane % LANES) // (HEAD_DIM // 2)
    zero = jnp.zeros_like(q)
    return jnp.concatenate([jnp.where(q_head == h, q, zero) for h in range(4)], axis=0)


def _pick_heads(parts, lane_head):
    out = parts[3]
    for h in (2, 1, 0):
        out = jnp.where(lane_head == h, parts[h], out)
    return out


def _dil_kernel(L, T, R, n_res, q_ref, k_ref, v_ref, o_ref, st_ref):
    KW = T + 2 * R
    lane_head = lax.broadcasted_iota(jnp.int32, (1, QUAD), 1) // HEAD_DIM
    stat_lane = lax.broadcasted_iota(jnp.int32, (1, STAT_LANES), 1)
    qi = lax.broadcasted_iota(jnp.int32, (T, KW), 0)
    kj = lax.broadcasted_iota(jnp.int32, (T, KW), 1)

    def body(i, carry):
        q0 = pl.multiple_of(i * T, T)
        ks = pl.multiple_of(jnp.clip(q0 - R, 0, L - KW), R)
        off = kj - qi + (ks - q0)
        bias = jnp.where(jnp.abs(off) <= R, 0.0, MASKED).astype(F32)
        bias = jnp.concatenate([bias] * 4, axis=0)
        for res in range(n_res):
            block(q0, ks, bias, slice(res * QUAD, (res + 1) * QUAD),
                  slice(res * STAT_LANES, (res + 1) * STAT_LANES))
        return carry

    def block(q0, ks, bias, lanes, stat_lanes):
        q = q_ref[pl.ds(q0, T), lanes]
        k = k_ref[pl.ds(ks, KW), lanes]
        v = v_ref[pl.ds(ks, KW), lanes]
        s = lax.dot_general(_stack_heads(q), k,(((1,), (1,)), ((), ())),
                            preferred_element_type=F32)
        s = s + bias
        m = jnp.max(s, axis=-1, keepdims=True)
        p = jnp.exp(s - m)
        den = jnp.sum(p, axis=-1, keepdims=True)
        o = jnp.dot(p.astype(BF16), v, preferred_element_type=F32) / den
        stat = jnp.zeros((T, STAT_LANES), F32)
        for h in range(4):
            stat = jnp.where(stat_lane == h, m[h * T:(h + 1) * T], stat)
            stat = jnp.where(stat_lane == 4 + h, den[h * T:(h + 1) * T], stat)
        outs = [o[h * T:(h + 1) * T] for h in range(4)]
        o_ref[pl.ds(q0, T), lanes] = _pick_heads(outs, lane_head).astype(o_ref.dtype)
        st_ref[pl.ds(q0, T), stat_lanes] = stat

    lax.fori_loop(0, L // T, body, 0, unroll=max(1, 8 // n_res))


def _dilated_call(q, k, v, B, S, group, window, dilation, T=128):
    d = dilation
    R = (window // 2) // d
    L = S // d
    assert L % T == 0 and L >= T + 2 * R and R % 16 == 0
    view = lambda t: t.reshape(B, L, d * QUAD)
    n_res = min(d, DIL_RESIDUES_PER_STEP)
    in_spec = pl.BlockSpec((None, L, n_res * QUAD), lambda b, r: (b, 0, r))
    o, st = pl.pallas_call(
        functools.partial(_dil_kernel, L, T, R, n_res),
        grid=(B, d // n_res),
        in_specs=[in_spec, in_spec, in_spec],
        out_specs=[pl.BlockSpec((None, L, n_res * QUAD), lambda b, r: (b, 0, r)),
                   pl.BlockSpec((None, L, n_res * STAT_LANES), lambda b, r: (b, 0, r))],
        out_shape=[jax.ShapeDtypeStruct((B, L, d * QUAD), BF16),
                   jax.ShapeDtypeStruct((B, L, d * STAT_LANES), F32)],
        compiler_params=_cparams(("parallel", "parallel")),
        name=f"dilated_g{group}",
    )(view(q), view(k), view(v))
    return o.reshape(B * L, d * QUAD), st.reshape(B * L, d * STAT_LANES)


def _na_kernel(rows, RB, q_ref, k_ref, v_ref, bias_ref, o_ref):
    lane_head = lax.broadcasted_iota(jnp.int32, (1, QUAD), 1) // HEAD_DIM
    n_keys = NA_KH * GRID_W
    i = pl.program_id(1)
    for rr in range(RB):
        r = i * RB + rr
        rs = jnp.clip(r - NA_KH // 2, 0, rows - NA_KH)
        e = rs - r + NA_KH - 1
        k0 = pl.multiple_of(rs * GRID_W, GRID_W)
        for quad in range(NA_WIDTH // QUAD):
            lanes = slice(quad * QUAD, (quad + 1) * QUAD)
            q = q_ref[rr * GRID_W:(rr + 1) * GRID_W, lanes]
            k = k_ref[pl.ds(k0, n_keys), lanes]
            v = v_ref[pl.ds(k0, n_keys), lanes]
            s = lax.dot_general(_stack_heads(q), k,(((1,), (1,)), ((), ())),
                                preferred_element_type=F32)
            s = s + bias_ref[quad, e]
            m = jnp.max(s, axis=-1, keepdims=True)
            p = jnp.exp(s - m)
            den = jnp.sum(p, axis=-1, keepdims=True)
            o = jnp.dot(p.astype(BF16), v, preferred_element_type=F32) / den
            outs = [o[h * GRID_W:(h + 1) * GRID_W] for h in range(4)]
            o_ref[rr * GRID_W:(rr + 1) * GRID_W, lanes] = _pick_heads(outs, lane_head).astype(o_ref.dtype)


def _na_bias_table(rpb):
    c = np.arange(GRID_W)
    ws = np.clip(c - NA_KW // 2, 0, GRID_W - NA_KW)
    col_ok = (c[None, :] >= ws[:, None]) & (c[None, :] < ws[:, None] + NA_KW)
    dc = np.clip(c[None, :] - c[:, None] + NA_KW - 1, 0, 2 * NA_KW - 2)
    n_dc = 2 * NA_KW - 1
    n_heads, n_dr = rpb.shape[0], rpb.shape[1]
    onehot = np.zeros((n_dc, GRID_W * GRID_W), np.float32)
    onehot[dc.reshape(-1), np.arange(GRID_W * GRID_W)] = 1.0
    a = jnp.dot(rpb.reshape(n_heads * n_dr, n_dc).astype(F32), onehot, precision=lax.Precision.HIGHEST)
    a = jnp.where(col_ok[None, None], a.reshape(n_heads, n_dr, GRID_W, GRID_W), MASKED)
    t = jnp.stack([a[:, e:e + NA_KH] for e in range(NA_KH)], axis=1)
    t = t.reshape(n_heads // 4, 4, NA_KH, NA_KH, GRID_W, GRID_W).transpose(0, 2, 1, 4, 3, 5)
    return t.reshape(n_heads // 4, NA_KH, 4 * GRID_W, NA_KH * GRID_W)


def _na_call(qn, kn, vn, bias, RB=16):
    B, S, _ = qn.shape
    rows = S // GRID_W
    assert rows >= NA_KH and rows % RB == 0
    blk = pl.BlockSpec((None, RB * GRID_W, NA_WIDTH), lambda b, i: (b, i, 0))
    full = pl.BlockSpec((None, S, NA_WIDTH), lambda b, i: (b, 0, 0))
    return pl.pallas_call(
        functools.partial(_na_kernel, rows, RB),
        grid=(B, rows // RB),
        in_specs=[blk, full, full, _const_spec(bias.shape)],
        out_specs=blk,
        out_shape=jax.ShapeDtypeStruct((B, S, NA_WIDTH), BF16),
        compiler_params=_cparams(("parallel", "arbitrary")),
        name="neighbourhood",
    )(qn, kn, vn, bias)


def _token_major(o_ref, s_ref, o_tmp, s_tmp, dil):
    if dil == 1:
        return o_ref[...].astype(F32), s_ref[...]
    rows = o_ref.shape[0]
    halves = QUAD // LANES
    for r in range(dil):
        blk = o_ref[:, r * QUAD:(r + 1) * QUAD].astype(F32)
        for half in range(halves):
            o_tmp[half, pl.ds(r, rows, stride=dil), :] = blk[:, half * LANES:(half + 1) * LANES]
        s_tmp[pl.ds(r, rows, stride=dil), :] = s_ref[:, r * STAT_LANES:(r + 1) * STAT_LANES]
    return jnp.concatenate([o_tmp[half] for half in range(halves)], axis=1), s_tmp[...]


def _mix_kernel(dils, x_ref, o0_ref, o1_ref, o2_ref, s0_ref, s1_ref, s2_ref, yb_ref, ga_ref, gn_ref,
                pa_ref, pb_ref, wo_ref, g2_ref, wrh_ref, wrl_ref, x1_ref, h2_ref, aff_ref, o_tmp, s_tmp):
    tile_outs, tile_stats = [], []
    for g, (o_ref, s_ref, dil) in enumerate(zip((o0_ref, o1_ref, o2_ref), (s0_ref, s1_ref, s2_ref), dils)):
        o, st = _token_major(o_ref, s_ref, o_tmp.at[g], s_tmp.at[g], dil)
        tile_outs.append(o)
        tile_stats.append(st)
    for r0 in range(0, x_ref.shape[0], MIX_SUB_ROWS):
        rows = slice(r0, r0 + MIX_SUB_ROWS)
        _mix_rows([o[rows] for o in tile_outs], [st[rows] for st in tile_stats],
                  x_ref.at[rows], yb_ref.at[rows], ga_ref.at[rows], gn_ref.at[rows],
                  pa_ref, pb_ref, wo_ref, g2_ref, wrh_ref, wrl_ref,
                  x1_ref.at[rows], h2_ref.at[rows], aff_ref.at[:, rows])


def _mix_rows(outs, stats, x_ref, yb_ref, ga_ref, gn_ref, pa_ref, pb_ref, wo_ref, g2_ref, wrh_ref, wrl_ref,
              x1_ref, h2_ref, aff_ref):
    lane_head = lax.broadcasted_iota(jnp.int32, (1, QUAD), 1) // HEAD_DIM
    coef = [None, None, None]
    for h in range(4):
        ms = [st[:, h:h + 1] for st in stats]
        dens = [st[:, 4 + h:5 + h] for st in stats]
        top = jnp.maximum(jnp.maximum(ms[0], ms[1]), ms[2])
        ws = [dens[g] * jnp.exp(ms[g] - top) for g in range(3)]
        tot = ws[0] + ws[1] + ws[2]
        for g in range(3):
            cg = ws[g] / tot
            coef[g] = cg if coef[g] is None else jnp.where(lane_head == h, cg, coef[g])
    ya = coef[0] * outs[0] + coef[1] * outs[1] + coef[2] * outs[2]
    ta = jnp.dot(ya.astype(BF16), pa_ref[...], preferred_element_type=F32)
    tb = jnp.dot(yb_ref[...], pb_ref[...], preferred_element_type=F32)
    merged = ga_ref[...].astype(F32) * ta + gn_ref[...].astype(F32) * tb
    x1 = x_ref[...] + jnp.dot(merged.astype(BF16), wo_ref[...], preferred_element_type=F32)
    x1_ref[...] = x1
    ms2 = jnp.mean(x1 * x1, axis=-1, keepdims=True)
    h2 = x1 * lax.rsqrt(ms2 + EPS) * g2_ref[...]
    h2_ref[...] = h2
    h_hi = h2.astype(BF16)
    h_lo = (h2 - h_hi.astype(F32)).astype(BF16)
    nt = (((1,), (1,)), ((), ()))
    both = lax.dot_general(jnp.concatenate([wrh_ref[...], wrl_ref[...]], axis=0), h_hi, nt,
                           preferred_element_type=F32)
    n_exp = wrh_ref.shape[0]
    logits = (both[:n_exp] + lax.dot_general(wrh_ref[...], h_lo, nt, preferred_element_type=F32)
              + both[n_exp:])
    z = jnp.exp(logits - jnp.max(logits, axis=0, keepdims=True))
    aff_ref[...] = z / jnp.sum(z, axis=0, keepdims=True)


def _mix_call(x2, o_parts, st_parts, yb, ga, gn, pa, pb, wo, g2, wr_hi, wr_lo, B, S, tm):
    M, D = x2.shape
    row = lambda i: (i, 0)
    blocks = S // tm
    rows_of = lambda w, d=1: pl.BlockSpec((tm // d, w * d), row)
    dils = tuple(d for _, d in DIL_PAIRS)
    return pl.pallas_call(
        functools.partial(_mix_kernel, dils),
        grid=(M // tm,),
        in_specs=[rows_of(D)] + [rows_of(QUAD, d) for d in dils] + [rows_of(STAT_LANES, d) for d in dils]
                 + [rows_of(NA_WIDTH), rows_of(D), rows_of(D),
                    _const_spec(pa.shape), _const_spec(pb.shape), _const_spec(wo.shape),
                    _const_spec(g2.shape), _const_spec(wr_hi.shape), _const_spec(wr_lo.shape)],
        out_specs=[rows_of(D), rows_of(D),
                   pl.BlockSpec((None, N_EXPERTS, tm), lambda i: (i // blocks, 0, i % blocks))],
        out_shape=[jax.ShapeDtypeStruct((M, D), F32), jax.ShapeDtypeStruct((M, D), F32),
                   jax.ShapeDtypeStruct((B, N_EXPERTS, S), F32)],
        scratch_shapes=[pltpu.VMEM((len(dils), QUAD // LANES, tm, LANES), F32),
                        pltpu.VMEM((len(dils), tm, LANES), F32)],
        compiler_params=_cparams(("parallel",)),
        name="mix",
    )(x2, *o_parts, *st_parts, yb, ga, gn, pa, pb, wo, g2, wr_hi, wr_lo)


def _select_kernel(cap, n_exp, rows, aff_ref, tri_ref, below_ref, idx_ref, gate_ref):
    n_bits = 31
    affs = [aff_ref[e * rows:(e + 1) * rows, :] for e in range(n_exp)]
    as_float = lambda word: pltpu.bitcast(word, F32)

    def count(mask):
        return jnp.sum(jnp.sum(mask.astype(F32), axis=1, keepdims=True), axis=0, keepdims=True)

    def refine(i, thrs):
        bit = lax.shift_left(jnp.int32(1), n_bits - 1 - i)
        out = []
        for e in range(n_exp):
            cand = thrs[e] | bit
            out.append(jnp.where(count(affs[e] >= as_float(cand)) >= cap, cand, thrs[e]))
        return tuple(out)

    thrs = lax.fori_loop(0, n_bits, refine, tuple(jnp.zeros((1, 1), jnp.int32) for _ in range(n_exp)))

    tri = tri_ref[...]
    below = below_ref[...]

    def running_count(mask):
        in_row = jnp.dot(mask.astype(BF16), tri, preferred_element_type=F32)
        row_tot = jnp.broadcast_to(in_row[:, LANES - 1:LANES], in_row.shape).astype(BF16)
        return in_row + jnp.dot(below, row_tot, preferred_element_type=F32)

    cuts = [as_float(t) for t in thrs]
    above = jnp.concatenate([affs[e] > cuts[e] for e in range(n_exp)], axis=0)
    equal = jnp.concatenate([affs[e] == cuts[e] for e in range(n_exp)], axis=0)
    room = jnp.concatenate(
        [jnp.broadcast_to(cap - count(affs[e] > cuts[e]), (rows, LANES)) for e in range(n_exp)], axis=0)
    chosen = above | (equal & (running_count(equal) <= room))
    cnt = running_count(chosen)

    lane = lax.broadcasted_iota(jnp.int32, (1, LANES), 1)
    lane_f = lane.astype(F32)
    slot = lax.broadcasted_iota(jnp.int32, (cap, 1), 0).astype(F32)
    diag = lax.broadcasted_iota(jnp.int32, (rows, LANES), 0) == lax.broadcasted_iota(jnp.int32, (rows, LANES), 1)
    pad = jnp.zeros((LANES - rows, LANES), BF16)
    idx_out = jnp.zeros((cap, LANES), jnp.int32)
    gate_out = jnp.zeros((cap, LANES), F32)
    for e in range(n_exp):
        cnt_e = cnt[e * rows:(e + 1) * rows]
        aff_e = aff_ref[e * rows:(e + 1) * rows, :]
        row_end = jnp.sum(jnp.where(diag, cnt_e[:, LANES - 1:LANES], 0.0), axis=0, keepdims=True)
        row_end = jnp.where(lane < rows, row_end, float(2 * cap))
        row_of = jnp.sum((row_end <= slot).astype(F32), axis=1, keepdims=True)
        pick = (lane_f == row_of).astype(BF16)
        cnt_hi = jnp.floor(cnt_e * (1.0 / 256.0))
        a_hi = aff_e.astype(BF16)
        a_r1 = aff_e - a_hi.astype(F32)
        a_mid = a_r1.astype(BF16)
        a_lo = (a_r1 - a_mid.astype(F32)).astype(BF16)
        pieces = [cnt_hi.astype(BF16), (cnt_e - 256.0 * cnt_hi).astype(BF16), a_hi, a_mid, a_lo]
        rhs = jnp.concatenate([jnp.concatenate([p, pad], axis=0) for p in pieces], axis=1)
        got = jnp.dot(pick, rhs, preferred_element_type=F32)
        row_cnt = 256.0 * got[:, 0:LANES] + got[:, LANES:2 * LANES]
        row_aff = got[:, 2 * LANES:3 * LANES] + got[:, 3 * LANES:4 * LANES] + got[:, 4 * LANES:5 * LANES]
        lane_of = jnp.sum((row_cnt <= slot).astype(F32), axis=1, keepdims=True)
        gate = jnp.sum(jnp.where(lane_f == lane_of, row_aff, 0.0), axis=1, keepdims=True)
        token = (row_of * float(LANES) + lane_of).astype(jnp.int32)
        idx_out = jnp.where(lane == e, token, idx_out)
        gate_out = jnp.where(lane == e, gate, gate_out)
    idx_ref[...] = idx_out
    gate_ref[...] = gate_out


def _select_call(aff, cap):
    B, E, S = aff.shape
    rows = S // LANES
    assert S % LANES == 0 and rows <= LANES and rows % 8 == 0 and cap <= 256 * 256
    r = np.arange(E * rows)
    tri = jnp.asarray(np.arange(LANES)[:, None] <= np.arange(LANES)[None, :], BF16)
    below = jnp.asarray((r[:, None] // rows == r[None, :] // rows) & (r[None, :] < r[:, None]), BF16)
    out_spec = pl.BlockSpec((None, cap, LANES), lambda b: (b, 0, 0))
    idx, gate = pl.pallas_call(
        functools.partial(_select_kernel, cap, E, rows),
        grid=(B,),
        in_specs=[pl.BlockSpec((None, E * rows, LANES), lambda b: (b, 0, 0)),
                  _const_spec(tri.shape), _const_spec(below.shape)],
        out_specs=[out_spec, out_spec],
        out_shape=[jax.ShapeDtypeStruct((B, cap, LANES), jnp.int32),
                   jax.ShapeDtypeStruct((B, cap, LANES), F32)],
        compiler_params=_cparams(("parallel",)),
        name="select",
    )(aff.reshape(B, E * rows, LANES), tri, below)
    unpack = lambda t: t[:, :, :E].transpose(0, 2, 1)
    return unpack(idx), unpack(gate)


def _moe_kernel(cap, S, D, idx_ref, gate_ref, x1_hbm, h2_hbm, wg_ref, wu_ref, wd_ref, out_hbm,
                acc_ref, h2_ref, xin_even_ref, xin_odd_ref, y_ref, slab_sem):
    n_slab = D // LANES
    seq_pitch = S + SLAB_PAD
    cap_pitch = cap + SLAB_PAD
    b = pl.program_id(0)
    e = pl.program_id(1)
    n_b = pl.num_programs(0)
    n_e = pl.num_programs(1)
    step = b * n_e + e
    last = n_b * n_e - 1

    def slab_copies(hbm, vmem_ref, seq, load, sem0=0):
        copies = []
        for k in range(n_slab):
            vm = vmem_ref.at[pl.ds(k * seq_pitch, S), :]
            hb = hbm.at[seq, :, pl.ds(k * LANES, LANES)]
            sem = slab_sem.at[sem0 + k]
            copies.append(pltpu.make_async_copy(hb, vm, sem) if load else pltpu.make_async_copy(vm, hb, sem))
        return copies

    def move(hbm, vmem_ref, seq, load):
        copies = slab_copies(hbm, vmem_ref, seq, load)
        for cp in copies:
            cp.start()
        for cp in copies:
            cp.wait()

    def gather_row(c, src_step, xin_ref):
        t = idx_ref[src_step * cap + c]
        xin_ref[pl.ds(c, n_slab, stride=cap_pitch), :] = h2_ref[pl.ds(t, n_slab, stride=seq_pitch), :]

    def scatter_rows(first, src_step, live):
        updates = []
        for i in range(SCATTER_BATCH):
            t = idx_ref[src_step * cap + first + i]
            g = gate_ref[src_step * cap + first + i] * live
            w = y_ref[pl.ds(first + i, n_slab, stride=cap_pitch), :] * g
            updates.append((t, acc_ref[pl.ds(t, n_slab, stride=seq_pitch), :] + w))
        for t, v in updates:
            acc_ref[pl.ds(t, n_slab, stride=seq_pitch), :] = v

    @pl.when(step == 0)
    def _():
        y_ref[...] = jnp.zeros_like(y_ref)
        loads = slab_copies(x1_hbm, acc_ref, b, True) + slab_copies(h2_hbm, h2_ref, b, True, sem0=n_slab)
        for cp in loads:
            cp.start()
        for cp in loads:
            cp.wait()

        def fill(c, carry):
            gather_row(c, step, xin_even_ref)
            return carry

        lax.fori_loop(0, cap, fill, 0)

    @pl.when((e == n_e - 1) & (b + 1 < n_b))
    def _():
        move(h2_hbm, h2_ref, b + 1, True)

    def pipeline_step(xin_ref, xin_next_ref):
        xb = jnp.concatenate([xin_ref[k * cap_pitch:k * cap_pitch + cap, :] for k in range(n_slab)],
                             axis=1).astype(BF16)
        nxt = jnp.minimum(step + 1, last)
        for c in range(cap):
            gather_row(c, nxt, xin_next_ref)
        a = jnp.dot(xb, wg_ref[...], preferred_element_type=F32)
        u = jnp.dot(xb, wu_ref[...], preferred_element_type=F32)
        hm = (a * (1.0 / (1.0 + jnp.exp(-a))) * u).astype(BF16)
        y = jnp.dot(hm, wd_ref[...], preferred_element_type=F32)
        prev = jnp.maximum(step - 1, 0)
        live = (step > 0).astype(F32)
        for first in range(0, cap, SCATTER_BATCH):
            scatter_rows(first, prev, live)
        for k in range(n_slab):
            y_ref[k * cap_pitch:k * cap_pitch + cap, :] = y[:, k * LANES:(k + 1) * LANES]

    @pl.when(step % 2 == 0)
    def _():
        pipeline_step(xin_even_ref, xin_odd_ref)

    @pl.when(step % 2 == 1)
    def _():
        pipeline_step(xin_odd_ref, xin_even_ref)

    @pl.when((e == 0) & (b > 0))
    def _():
        stores = slab_copies(out_hbm, acc_ref, b - 1, False)
        loads = slab_copies(x1_hbm, acc_ref, b, True, sem0=n_slab)
        for cp in stores:
            cp.start()
        for st, ld in zip(stores, loads):
            st.wait()
            ld.start()
        for cp in loads:
            cp.wait()

    @pl.when(step == last)
    def _():
        def drain(j, carry):
            scatter_rows(j * SCATTER_BATCH, step, jnp.float32(1.0))
            return carry

        lax.fori_loop(0, cap // SCATTER_BATCH, drain, 0)
        move(out_hbm, acc_ref, b, False)


def _moe_call(idx_flat, gate_flat, x1, h2, wg, wu, wd, cap):
    B, S, D = x1.shape
    E, _, FF = wg.shape
    assert D % LANES == 0 and cap % SCATTER_BATCH == 0
    n_slab = D // LANES
    smem = pl.BlockSpec(memory_space=pltpu.SMEM)
    hbm = pl.BlockSpec(memory_space=pl.ANY)
    return pl.pallas_call(
        functools.partial(_moe_kernel, cap, S, D),
        grid=(B, E),
        in_specs=[smem, smem, hbm, hbm,
                  pl.BlockSpec((None, D, FF), lambda b, e: (e, 0, 0)),
                  pl.BlockSpec((None, D, FF), lambda b, e: (e, 0, 0)),
                  pl.BlockSpec((None, FF, D), lambda b, e: (e, 0, 0))],
        out_specs=hbm,
        out_shape=jax.ShapeDtypeStruct((B, S, D), F32),
        scratch_shapes=[pltpu.VMEM((n_slab * (S + SLAB_PAD), LANES), F32),
                        pltpu.VMEM((n_slab * (S + SLAB_PAD), LANES), F32),
                        pltpu.VMEM((n_slab * (cap + SLAB_PAD), LANES), F32),
                        pltpu.VMEM((n_slab * (cap + SLAB_PAD), LANES), F32),
                        pltpu.VMEM((n_slab * (cap + SLAB_PAD), LANES), F32),
                        pltpu.SemaphoreType.DMA((2 * n_slab,))],
        compiler_params=_cparams(("arbitrary", "arbitrary")),
        name="moe",
    )(idx_flat, gate_flat, x1, h2, wg, wu, wd)


def _rope_tables(S):
    half = HEAD_DIM // 2
    inv = ROPE_THETA ** (-jnp.arange(half, dtype=F32) / half)
    ang = jnp.arange(S, dtype=F32)[:, None] * inv[None, :]
    cos = jnp.cos(ang)
    sin = jnp.sin(ang)
    heads = LANE_CHUNK // HEAD_DIM
    return (jnp.tile(cos, (1, 2 * heads)),
            jnp.concatenate([jnp.tile(-sin, (1, heads)), jnp.tile(sin, (1, heads))], axis=-1))


def _split_half_columns(w, start, width):
    rows = w.shape[0]
    seg = w[:, start:start + width].reshape(rows, width // QUAD, 4, 2, HEAD_DIM // 2)
    seg = seg.transpose(0, 1, 3, 2, 4).reshape(rows, width)
    return jnp.concatenate([w[:, :start], seg, w[:, start + width:]], axis=1)


def _layer(x, norm1_g, w_in, dil_q_g, dil_k_g, na_q_g, na_k_g, na_rpb, w_dil_branch, w_na_branch,
           w_out, norm2_g, w_router, w_gate, w_up, w_down):
    B, S, D = x.shape
    M = B * S
    tm = 512
    scale = HEAD_DIM ** -0.5
    dil_heads = DIL_WIDTH // HEAD_DIM
    half = HEAD_DIM // 2

    def quad_gains(g, n_heads):
        quad = jnp.concatenate([jnp.tile(g[:half], 4), jnp.tile(g[half:], 4)])
        return jnp.tile(quad, n_heads // 4)

    gain_cols = jnp.concatenate([
        quad_gains(dil_q_g * scale, dil_heads), quad_gains(dil_k_g, dil_heads), jnp.ones((DIL_WIDTH,), F32),
        quad_gains(na_q_g * scale, NA_HEADS), quad_gains(na_k_g, NA_HEADS), jnp.ones((NA_WIDTH + 2 * D,), F32),
    ])[None, :].astype(F32)
    cos_t, sin_t = _rope_tables(S)
    head_of_lane = _split_half_order() // HEAD_DIM
    bd = jnp.asarray(head_of_lane[:, None] == head_of_lane[None, :], BF16)
    w_cols = _split_half_columns(w_in, 0, 2 * DIL_WIDTH)
    w_cols = _split_half_columns(w_cols, 3 * DIL_WIDTH, 2 * NA_WIDTH)

    x2 = x.reshape(M, D)
    proj = _proj_call(x2, norm1_g[None, :], w_cols.astype(BF16), gain_cols, cos_t, sin_t, bd, S, D, tm)
    n_groups = len(DIL_PAIRS)
    qa, ka, va = proj[:n_groups], proj[n_groups:2 * n_groups], proj[2 * n_groups:3 * n_groups]
    qn, kn, vn, ga, gn = proj[3 * n_groups:]

    o_parts, st_parts = [], []
    for g, (window, dilation) in enumerate(DIL_PAIRS):
        o, st = _dilated_call(qa[g], ka[g], va[g], B, S, g, window, dilation)
        o_parts.append(o)
        st_parts.append(st)

    yb = _na_call(qn.reshape(B, S, -1), kn.reshape(B, S, -1), vn.reshape(B, S, -1),
                  _na_bias_table(na_rpb)).reshape(M, NA_WIDTH)

    wr_t = w_router.T
    wr_hi = wr_t.astype(BF16)
    wr_lo = (wr_t - wr_hi.astype(F32)).astype(BF16)
    x1, h2, aff = _mix_call(x2, o_parts, st_parts, yb, ga, gn,
                            w_dil_branch.astype(BF16), w_na_branch.astype(BF16), w_out.astype(BF16),
                            norm2_g[None, :], wr_hi, wr_lo, B, S, MIX_TILE_ROWS)

    cap = (EC_CAPACITY_FACTOR * S) // N_EXPERTS
    idx, gates = _select_call(aff, cap)
    return _moe_call(idx.reshape(-1), gates.reshape(-1),
                     x1.reshape(B, S, D), h2.reshape(B, S, D),
                     w_gate.astype(BF16), w_up.astype(BF16), w_down.astype(BF16), cap)


def kernel(x, norm1_g, w_in, dil_q_norm_g, dil_k_norm_g, na_q_norm_g, na_k_norm_g, na_rpb, w_dil_branch,
           w_na_branch, w_out, norm2_g, w_router, w_gate, w_up, w_down):
    for l in range(norm1_g.shape[0]):
        x = _layer(x, norm1_g[l], w_in[l], dil_q_norm_g[l], dil_k_norm_g[l], na_q_norm_g[l],
                   na_k_norm_g[l], na_rpb[l], w_dil_branch[l], w_na_branch[l], w_out[l], norm2_g[l],
                   w_router[l], w_gate[l], w_up[l], w_down[l])
    return x
```

```python
import functools
import math

import jax
import jax.numpy as jnp
import numpy as np
from jax import lax
from jax.experimental import pallas as pl
from jax.experimental.pallas import tpu as pltpu

F32 = jnp.float32
BF16 = jnp.bfloat16

HEAD_DIM = 64
DIL_PAIRS = ((128, 1), (512, 4), (2048, 16))
DIL_GROUP_HEADS = 4
DIL_WIDTH = DIL_GROUP_HEADS * len(DIL_PAIRS) * HEAD_DIM
DIL_OUT = DIL_GROUP_HEADS * HEAD_DIM
NA_HEADS = 8
NA_WIDTH = NA_HEADS * HEAD_DIM
GRID_W = 64
NA_KH = 8
NA_KW = 16
N_EXPERTS = 16
EC_CAPACITY_FACTOR = 2
ROPE_THETA = 10000.0
EPS = 1e-6
MASKED = -1e30

LANE_CHUNK = 256
QUAD = 4 * HEAD_DIM
STAT_LANES = 128
LANES = 128
SLAB_PAD = 8
SCATTER_BATCH = 4
DIL_RESIDUES_PER_STEP = 4
MIX_TILE_ROWS = 1024
MIX_SUB_ROWS = 512
VMEM_LIMIT = 56 * 1024 * 1024


def _cparams(sem):
    return pltpu.CompilerParams(dimension_semantics=sem, vmem_limit_bytes=VMEM_LIMIT)


def _const_spec(shape):
    nd = len(shape)
    return pl.BlockSpec(shape, lambda *_: (0,) * nd, pipeline_mode=pl.Buffered(1))


def _proj_kernel(segs, x_ref, g1_ref, w_ref, gain_ref, cos_ref, sin_ref, bd_ref, *out_and_scratch):
    *out_refs, tmp_ref = out_and_scratch
    x = x_ref[...]
    ms = jnp.mean(x * x, axis=-1, keepdims=True)
    h = (x * lax.rsqrt(ms + EPS) * g1_ref[...]).astype(BF16)
    bd = bd_ref[...]

    def rope(y):
        partner = jnp.concatenate([y[:, LANES:], y[:, :LANES]], axis=1)
        return y * cos_ref[...] + partner * sin_ref[...]

    tm = x.shape[0]
    chunks, col = [], 0
    for ref, (width, kind, dil) in zip(out_refs, segs):
        for c in range(0, width, LANE_CHUNK):
            chunks.append((ref, c, kind, dil, col + c))
        col += width
    raw = {}
    for norm_kind in ("qk_rope", "qk"):
        normed = [ch for ch in chunks if ch[2] == norm_kind]
        for ch in normed:
            raw[ch[4]] = jnp.dot(h, w_ref[:, ch[4]:ch[4] + LANE_CHUNK], preferred_element_type=F32)
        squares = jnp.concatenate([(raw[ch[4]] * raw[ch[4]]).astype(BF16) for ch in normed], axis=0)
        sumsq = jnp.dot(squares, bd, preferred_element_type=F32)
        for i, ch in enumerate(normed):
            ss = sumsq[i * tm:(i + 1) * tm]
            raw[ch[4]] = (raw[ch[4]] * lax.rsqrt(ss * (1.0 / HEAD_DIM) + EPS)
                          * gain_ref[:, ch[4]:ch[4] + LANE_CHUNK])

    for ref, c, kind, dil, c_abs in chunks:
        if kind == "qk_rope":
            p = rope(raw[c_abs])
        elif kind == "qk":
            p = raw[c_abs]
        else:
            p = jnp.dot(h, w_ref[:, c_abs:c_abs + LANE_CHUNK], preferred_element_type=F32)
        if kind == "gate":
            p = 1.0 / (1.0 + jnp.exp(-p))
        if dil == 1:
            ref[:, c:c + LANE_CHUNK] = p.astype(ref.dtype)
        else:
            for half in range(LANE_CHUNK // LANES):
                tmp_ref[half] = p[:, half * LANES:(half + 1) * LANES]
            for r in range(dil):
                for half in range(LANE_CHUNK // LANES):
                    c0 = r * LANE_CHUNK + half * LANES
                    ref[:, c0:c0 + LANES] = tmp_ref[half, pl.ds(r, tm // dil, stride=dil), :].astype(ref.dtype)


def _proj_call(x2, g1, w_in, gain_cols, cos_t, sin_t, bd, S, d_model, tm):
    M = x2.shape[0]
    dils = [d for _, d in DIL_PAIRS]
    segs = tuple((QUAD, kind, d) for kind in ("qk_rope", "qk_rope", "v") for d in dils) + (
        (NA_WIDTH, "qk", 1), (NA_WIDTH, "qk", 1), (NA_WIDTH, "v", 1),
        (d_model, "gate", 1), (d_model, "gate", 1))
    n_cols = sum(w for w, _, _ in segs)
    pos_blocks = S // tm
    row = lambda i: (i, 0)

    return pl.pallas_call(
        functools.partial(_proj_kernel, segs),
        grid=(M // tm,),
        in_specs=[
            pl.BlockSpec((tm, d_model), row),
            _const_spec((1, d_model)),
            _const_spec((d_model, n_cols)),
            _const_spec((1, n_cols)),
            pl.BlockSpec((tm, LANE_CHUNK), lambda i: (i % pos_blocks, 0)),
            pl.BlockSpec((tm, LANE_CHUNK), lambda i: (i % pos_blocks, 0)),
            _const_spec((LANE_CHUNK, LANE_CHUNK)),
        ],
        out_specs=[pl.BlockSpec((tm // d, w * d), row) for w, _, d in segs],
        out_shape=[jax.ShapeDtypeStruct((M // d, w * d), BF16) for w, _, d in segs],
        scratch_shapes=[pltpu.VMEM((LANE_CHUNK // LANES, tm, LANES), F32)],
        compiler_params=_cparams(("parallel",)),
        name="proj",
    )(x2, g1, w_in, gain_cols, cos_t, sin_t, bd)


def _split_half_order():
    n = np.arange(QUAD)
    half, head, i = n // LANES, (n % LANES) // (HEAD_DIM // 2), n % (HEAD_DIM // 2)
    return head * HEAD_DIM + half * (HEAD_DIM // 2) + i


def _stack_heads(q):
    lane = lax.broadcasted_iota(jnp.int32, (1, QUAD), 1)
    q_head = (lane % LANES) // (HEAD_DIM // 2)
    zero = jnp.zeros_like(q)
    return jnp.concatenate([jnp.where(q_head == h, q, zero) for h in range(4)], axis=0)


def _pick_heads(parts, lane_head):
    out = parts[3]
    for h in (2, 1, 0):
        out = jnp.where(lane_head == h, parts[h], out)
    return out


def _dil_kernel(L, T, R, n_res, q_ref, k_ref, v_ref, o_ref, st_ref):
    KW = T + 2 * R
    lane_head = lax.broadcasted_iota(jnp.int32, (1, QUAD), 1) // HEAD_DIM
    stat_lane = lax.broadcasted_iota(jnp.int32, (1, STAT_LANES), 1)
    qi = lax.broadcasted_iota(jnp.int32, (T, KW), 0)
    kj = lax.broadcasted_iota(jnp.int32, (T, KW), 1)

    def body(i, carry):
        q0 = pl.multiple_of(i * T, T)
        ks = pl.multiple_of(jnp.clip(q0 - R, 0, L - KW), R)
        off = kj - qi + (ks - q0)
        bias = jnp.where(jnp.abs(off) <= R, 0.0, MASKED).astype(F32)
        bias = jnp.concatenate([bias] * 4, axis=0)
        for res in range(n_res):
            block(q0, ks, bias, slice(res * QUAD, (res + 1) * QUAD),
                  slice(res * STAT_LANES, (res + 1) * STAT_LANES))
        return carry

    def block(q0, ks, bias, lanes, stat_lanes):
        q = q_ref[pl.ds(q0, T), lanes]
        k = k_ref[pl.ds(ks, KW), lanes]
        v = v_ref[pl.ds(ks, KW), lanes]
        s = lax.dot_general(_stack_heads(q), k,(((1,), (1,)), ((), ())),
                            preferred_element_type=F32)
        s = s + bias
        m = jnp.max(s, axis=-1, keepdims=True)
        p = jnp.exp(s - m)
        den = jnp.sum(p, axis=-1, keepdims=True)
        o = jnp.dot(p.astype(BF16), v, preferred_element_type=F32) / den
        stat = jnp.zeros((T, STAT_LANES), F32)
        for h in range(4):
            stat = jnp.where(stat_lane == h, m[h * T:(h + 1) * T], stat)
            stat = jnp.where(stat_lane == 4 + h, den[h * T:(h + 1) * T], stat)
        outs = [o[h * T:(h + 1) * T] for h in range(4)]
        o_ref[pl.ds(q0, T), lanes] = _pick_heads(outs, lane_head).astype(o_ref.dtype)
        st_ref[pl.ds(q0, T), stat_lanes] = stat

    lax.fori_loop(0, L // T, body, 0, unroll=max(1, min(L // T, 16 // n_res)))


def _dilated_call(q, k, v, B, S, group, window, dilation, T=128):
    d = dilation
    R = (window // 2) // d
    L = S // d
    assert L % T == 0 and L >= T + 2 * R and R % 16 == 0
    view = lambda t: t.reshape(B, L, d * QUAD)
    n_res = min(d, DIL_RESIDUES_PER_STEP)
    in_spec = pl.BlockSpec((None, L, n_res * QUAD), lambda b, r: (b, 0, r))
    o, st = pl.pallas_call(
        functools.partial(_dil_kernel, L, T, R, n_res),
        grid=(B, d // n_res),
        in_specs=[in_spec, in_spec, in_spec],
        out_specs=[pl.BlockSpec((None, L, n_res * QUAD), lambda b, r: (b, 0, r)),
                   pl.BlockSpec((None, L, n_res * STAT_LANES), lambda b, r: (b, 0, r))],
        out_shape=[jax.ShapeDtypeStruct((B, L, d * QUAD), BF16),
                   jax.ShapeDtypeStruct((B, L, d * STAT_LANES), F32)],
        compiler_params=_cparams(("parallel", "parallel")),
        name=f"dilated_g{group}",
    )(view(q), view(k), view(v))
    return o.reshape(B * L, d * QUAD), st.reshape(B * L, d * STAT_LANES)


def _na_kernel(rows, RB, q_ref, k_ref, v_ref, bias_ref, o_ref):
    lane_head = lax.broadcasted_iota(jnp.int32, (1, QUAD), 1) // HEAD_DIM
    n_keys = NA_KH * GRID_W
    i = pl.program_id(1)
    for rr in range(RB):
        r = i * RB + rr
        rs = jnp.clip(r - NA_KH // 2, 0, rows - NA_KH)
        e = rs - r + NA_KH - 1
        k0 = pl.multiple_of(rs * GRID_W, GRID_W)
        for quad in range(NA_WIDTH // QUAD):
            lanes = slice(quad * QUAD, (quad + 1) * QUAD)
            q = q_ref[rr * GRID_W:(rr + 1) * GRID_W, lanes]
            k = k_ref[pl.ds(k0, n_keys), lanes]
            v = v_ref[pl.ds(k0, n_keys), lanes]
            s = lax.dot_general(_stack_heads(q), k,(((1,), (1,)), ((), ())),
                                preferred_element_type=F32)
            s = s + bias_ref[quad, e]
            m = jnp.max(s, axis=-1, keepdims=True)
            p = jnp.exp(s - m)
            den = jnp.sum(p, axis=-1, keepdims=True)
            o = jnp.dot(p.astype(BF16), v, preferred_element_type=F32) / den
            outs = [o[h * GRID_W:(h + 1) * GRID_W] for h in range(4)]
            o_ref[rr * GRID_W:(rr + 1) * GRID_W, lanes] = _pick_heads(outs, lane_head).astype(o_ref.dtype)


def _na_bias_table(rpb):
    c = np.arange(GRID_W)
    ws = np.clip(c - NA_KW // 2, 0, GRID_W - NA_KW)
    col_ok = (c[None, :] >= ws[:, None]) & (c[None, :] < ws[:, None] + NA_KW)
    dc = np.clip(c[None, :] - c[:, None] + NA_KW - 1, 0, 2 * NA_KW - 2)
    n_dc = 2 * NA_KW - 1
    n_heads, n_dr = rpb.shape[0], rpb.shape[1]
    onehot = np.zeros((n_dc, GRID_W * GRID_W), np.float32)
    onehot[dc.reshape(-1), np.arange(GRID_W * GRID_W)] = 1.0
    a = jnp.dot(rpb.reshape(n_heads * n_dr, n_dc).astype(F32), onehot, precision=lax.Precision.HIGHEST)
    a = jnp.where(col_ok[None, None], a.reshape(n_heads, n_dr, GRID_W, GRID_W), MASKED)
    t = jnp.stack([a[:, e:e + NA_KH] for e in range(NA_KH)], axis=1)
    t = t.reshape(n_heads // 4, 4, NA_KH, NA_KH, GRID_W, GRID_W).transpose(0, 2, 1, 4, 3, 5)
    return t.reshape(n_heads // 4, NA_KH, 4 * GRID_W, NA_KH * GRID_W)


def _na_call(qn, kn, vn, bias, RB=16):
    B, S, _ = qn.shape
    rows = S // GRID_W
    assert rows >= NA_KH and rows % RB == 0
    blk = pl.BlockSpec((None, RB * GRID_W, NA_WIDTH), lambda b, i: (b, i, 0))
    full = pl.BlockSpec((None, S, NA_WIDTH), lambda b, i: (b, 0, 0))
    return pl.pallas_call(
        functools.partial(_na_kernel, rows, RB),
        grid=(B, rows // RB),
        in_specs=[blk, full, full, _const_spec(bias.shape)],
        out_specs=blk,
        out_shape=jax.ShapeDtypeStruct((B, S, NA_WIDTH), BF16),
        compiler_params=_cparams(("parallel", "arbitrary")),
        name="neighbourhood",
    )(qn, kn, vn, bias)


def _token_major(o_ref, s_ref, o_tmp, s_tmp, dil):
    if dil == 1:
        return o_ref[...].astype(F32), s_ref[...]
    rows = o_ref.shape[0]
    halves = QUAD // LANES
    for r in range(dil):
        blk = o_ref[:, r * QUAD:(r + 1) * QUAD].astype(F32)
        for half in range(halves):
            o_tmp[half, pl.ds(r, rows, stride=dil), :] = blk[:, half * LANES:(half + 1) * LANES]
        s_tmp[pl.ds(r, rows, stride=dil), :] = s_ref[:, r * STAT_LANES:(r + 1) * STAT_LANES]
    return jnp.concatenate([o_tmp[half] for half in range(halves)], axis=1), s_tmp[...]


def _mix_kernel(dils, x_ref, o0_ref, o1_ref, o2_ref, s0_ref, s1_ref, s2_ref, yb_ref, ga_ref, gn_ref,
                pa_ref, pb_ref, wo_ref, g2_ref, wrh_ref, wrl_ref, x1_ref, h2_ref, aff_ref, o_tmp, s_tmp):
    tile_outs, tile_stats = [], []
    for g, (o_ref, s_ref, dil) in enumerate(zip((o0_ref, o1_ref, o2_ref), (s0_ref, s1_ref, s2_ref), dils)):
        o, st = _token_major(o_ref, s_ref, o_tmp.at[g], s_tmp.at[g], dil)
        tile_outs.append(o)
        tile_stats.append(st)
    for r0 in range(0, x_ref.shape[0], MIX_SUB_ROWS):
        rows = slice(r0, r0 + MIX_SUB_ROWS)
        _mix_rows([o[rows] for o in tile_outs], [st[rows] for st in tile_stats],
                  x_ref.at[rows], yb_ref.at[rows], ga_ref.at[rows], gn_ref.at[rows],
                  pa_ref, pb_ref, wo_ref, g2_ref, wrh_ref, wrl_ref,
                  x1_ref.at[rows], h2_ref.at[rows], aff_ref, r0 // LANES)


def _mix_rows(outs, stats, x_ref, yb_ref, ga_ref, gn_ref, pa_ref, pb_ref, wo_ref, g2_ref, wrh_ref, wrl_ref,
              x1_ref, h2_ref, aff_ref, first_group):
    lane_head = lax.broadcasted_iota(jnp.int32, (1, QUAD), 1) // HEAD_DIM
    coef = [None, None, None]
    for h in range(4):
        ms = [st[:, h:h + 1] for st in stats]
        dens = [st[:, 4 + h:5 + h] for st in stats]
        top = jnp.maximum(jnp.maximum(ms[0], ms[1]), ms[2])
        ws = [dens[g] * jnp.exp(ms[g] - top) for g in range(3)]
        tot = ws[0] + ws[1] + ws[2]
        for g in range(3):
            cg = ws[g] / tot
            coef[g] = cg if coef[g] is None else jnp.where(lane_head == h, cg, coef[g])
    ya = coef[0] * outs[0] + coef[1] * outs[1] + coef[2] * outs[2]
    ta = jnp.dot(ya.astype(BF16), pa_ref[...], preferred_element_type=F32)
    tb = jnp.dot(yb_ref[...], pb_ref[...], preferred_element_type=F32)
    merged = ga_ref[...].astype(F32) * ta + gn_ref[...].astype(F32) * tb
    x1 = x_ref[...] + jnp.dot(merged.astype(BF16), wo_ref[...], preferred_element_type=F32)
    x1_ref[...] = x1
    ms2 = jnp.mean(x1 * x1, axis=-1, keepdims=True)
    h2 = x1 * lax.rsqrt(ms2 + EPS) * g2_ref[...]
    h2_ref[...] = h2
    h_hi = h2.astype(BF16)
    h_lo = (h2 - h_hi.astype(F32)).astype(BF16)
    nt = (((1,), (1,)), ((), ()))
    both = lax.dot_general(jnp.concatenate([wrh_ref[...], wrl_ref[...]], axis=0), h_hi, nt,
                           preferred_element_type=F32)
    n_exp = wrh_ref.shape[0]
    logits = (both[:n_exp] + lax.dot_general(wrh_ref[...], h_lo, nt, preferred_element_type=F32)
              + both[n_exp:])
    z = jnp.exp(logits - jnp.max(logits, axis=0, keepdims=True))
    aff = z / jnp.sum(z, axis=0, keepdims=True)
    groups = aff_ref.shape[0] // n_exp
    for j in range(aff.shape[1] // LANES):
        aff_ref[pl.ds(first_group + j, n_exp, stride=groups), :] = aff[:, j * LANES:(j + 1) * LANES]


def _mix_call(x2, o_parts, st_parts, yb, ga, gn, pa, pb, wo, g2, wr_hi, wr_lo, B, S, tm):
    M, D = x2.shape
    row = lambda i: (i, 0)
    blocks = S // tm
    rows_of = lambda w, d=1: pl.BlockSpec((tm // d, w * d), row)
    dils = tuple(d for _, d in DIL_PAIRS)
    return pl.pallas_call(
        functools.partial(_mix_kernel, dils),
        grid=(M // tm,),
        in_specs=[rows_of(D)] + [rows_of(QUAD, d) for d in dils] + [rows_of(STAT_LANES, d) for d in dils]
                 + [rows_of(NA_WIDTH), rows_of(D), rows_of(D),
                    _const_spec(pa.shape), _const_spec(pb.shape), _const_spec(wo.shape),
                    _const_spec(g2.shape), _const_spec(wr_hi.shape), _const_spec(wr_lo.shape)],
        out_specs=[rows_of(D), rows_of(D),
                   pl.BlockSpec((None, None, N_EXPERTS * (tm // LANES), LANES),
                                lambda i: (i // blocks, i % blocks, 0, 0))],
        out_shape=[jax.ShapeDtypeStruct((M, D), F32), jax.ShapeDtypeStruct((M, D), F32),
                   jax.ShapeDtypeStruct((B, blocks, N_EXPERTS * (tm // LANES), LANES), F32)],
        scratch_shapes=[pltpu.VMEM((len(dils), QUAD // LANES, tm, LANES), F32),
                        pltpu.VMEM((len(dils), tm, LANES), F32)],
        compiler_params=_cparams(("parallel",)),
        name="mix",
    )(x2, *o_parts, *st_parts, yb, ga, gn, pa, pb, wo, g2, wr_hi, wr_lo)


def _select_kernel(cap, n_exp, tiles, groups, aff_ref, tri_ref, below_ref, idx_ref, gate_ref):
    n_bits = 31
    rows = tiles * groups
    affs = [jnp.concatenate([aff_ref[(t * n_exp + e) * groups:(t * n_exp + e + 1) * groups, :]
                             for t in range(tiles)], axis=0) for e in range(n_exp)]
    as_float = lambda word: pltpu.bitcast(word, F32)

    def count(mask):
        return jnp.sum(jnp.sum(mask.astype(F32), axis=1, keepdims=True), axis=0, keepdims=True)

    def refine(i, thrs):
        bit = lax.shift_left(jnp.int32(1), n_bits - 1 - i)
        out = []
        for e in range(n_exp):
            cand = thrs[e] | bit
            out.append(jnp.where(count(affs[e] >= as_float(cand)) >= cap, cand, thrs[e]))
        return tuple(out)

    thrs = lax.fori_loop(0, n_bits, refine, tuple(jnp.zeros((1, 1), jnp.int32) for _ in range(n_exp)))

    tri = tri_ref[...]
    below = below_ref[...]

    def running_count(mask):
        in_row = jnp.dot(mask.astype(BF16), tri, preferred_element_type=F32)
        row_tot = jnp.broadcast_to(in_row[:, LANES - 1:LANES], in_row.shape).astype(BF16)
        return in_row + jnp.dot(below, row_tot, preferred_element_type=F32)

    cuts = [as_float(t) for t in thrs]
    above = jnp.concatenate([affs[e] > cuts[e] for e in range(n_exp)], axis=0)
    equal = jnp.concatenate([affs[e] == cuts[e] for e in range(n_exp)], axis=0)
    room = jnp.concatenate(
        [jnp.broadcast_to(cap - count(affs[e] > cuts[e]), (rows, LANES)) for e in range(n_exp)], axis=0)
    chosen = above | (equal & (running_count(equal) <= room))
    cnt = running_count(chosen)

    lane = lax.broadcasted_iota(jnp.int32, (1, LANES), 1)
    lane_f = lane.astype(F32)
    slot = lax.broadcasted_iota(jnp.int32, (cap, 1), 0).astype(F32)
    diag = lax.broadcasted_iota(jnp.int32, (rows, LANES), 0) == lax.broadcasted_iota(jnp.int32, (rows, LANES), 1)
    pad = jnp.zeros((LANES - rows, LANES), BF16)
    idx_out = jnp.zeros((cap, LANES), jnp.int32)
    gate_out = jnp.zeros((cap, LANES), F32)
    for e in range(n_exp):
        cnt_e = cnt[e * rows:(e + 1) * rows]
        aff_e = affs[e]
        row_end = jnp.sum(jnp.where(diag, cnt_e[:, LANES - 1:LANES], 0.0), axis=0, keepdims=True)
        row_end = jnp.where(lane < rows, row_end, float(2 * cap))
        row_of = jnp.sum((row_end <= slot).astype(F32), axis=1, keepdims=True)
        pick = (lane_f == row_of).astype(BF16)
        cnt_hi = jnp.floor(cnt_e * (1.0 / 256.0))
        a_hi = aff_e.astype(BF16)
        a_r1 = aff_e - a_hi.astype(F32)
        a_mid = a_r1.astype(BF16)
        a_lo = (a_r1 - a_mid.astype(F32)).astype(BF16)
        pieces = [cnt_hi.astype(BF16), (cnt_e - 256.0 * cnt_hi).astype(BF16), a_hi, a_mid, a_lo]
        rhs = jnp.concatenate([jnp.concatenate([p, pad], axis=0) for p in pieces], axis=1)
        got = jnp.dot(pick, rhs, preferred_element_type=F32)
        row_cnt = 256.0 * got[:, 0:LANES] + got[:, LANES:2 * LANES]
        row_aff = got[:, 2 * LANES:3 * LANES] + got[:, 3 * LANES:4 * LANES] + got[:, 4 * LANES:5 * LANES]
        lane_of = jnp.sum((row_cnt <= slot).astype(F32), axis=1, keepdims=True)
        gate = jnp.sum(jnp.where(lane_f == lane_of, row_aff, 0.0), axis=1, keepdims=True)
        token = (row_of * float(LANES) + lane_of).astype(jnp.int32)
        idx_out = jnp.where(lane == e, token, idx_out)
        gate_out = jnp.where(lane == e, gate, gate_out)
    idx_ref[...] = idx_out
    gate_ref[...] = gate_out


def _select_call(aff, E, cap):
    B, tiles, eg, _ = aff.shape
    groups = eg // E
    rows = tiles * groups
    assert groups % 8 == 0 and rows <= LANES and cap <= 256 * 256
    r = np.arange(E * rows)
    tri = jnp.asarray(np.arange(LANES)[:, None] <= np.arange(LANES)[None, :], BF16)
    below = jnp.asarray((r[:, None] // rows == r[None, :] // rows) & (r[None, :] < r[:, None]), BF16)
    out_spec = pl.BlockSpec((None, cap, LANES), lambda b: (b, 0, 0))
    idx, gate = pl.pallas_call(
        functools.partial(_select_kernel, cap, E, tiles, groups),
        grid=(B,),
        in_specs=[pl.BlockSpec((None, E * rows, LANES), lambda b: (b, 0, 0)),
                  _const_spec(tri.shape), _const_spec(below.shape)],
        out_specs=[out_spec, out_spec],
        out_shape=[jax.ShapeDtypeStruct((B, cap, LANES), jnp.int32),
                   jax.ShapeDtypeStruct((B, cap, LANES), F32)],
        compiler_params=_cparams(("parallel",)),
        name="select",
    )(aff.reshape(B, E * rows, LANES), tri, below)
    unpack = lambda t: t[:, :, :E].transpose(0, 2, 1)
    return unpack(idx), unpack(gate)


def _moe_kernel(cap, S, D, idx_ref, gate_ref, x1_hbm, h2_hbm, wg_ref, wu_ref, wd_ref, out_hbm,
                acc_ref, h2_ref, xin_even_ref, xin_odd_ref, y_ref, slab_sem):
    n_slab = D // LANES
    seq_pitch = S + SLAB_PAD
    cap_pitch = cap + SLAB_PAD
    b = pl.program_id(0)
    e = pl.program_id(1)
    n_b = pl.num_programs(0)
    n_e = pl.num_programs(1)
    step = b * n_e + e
    last = n_b * n_e - 1

    def slab_copies(hbm, vmem_ref, seq, load, sem0=0):
        copies = []
        for k in range(n_slab):
            vm = vmem_ref.at[pl.ds(k * seq_pitch, S), :]
            hb = hbm.at[seq, :, pl.ds(k * LANES, LANES)]
            sem = slab_sem.at[sem0 + k]
            copies.append(pltpu.make_async_copy(hb, vm, sem) if load else pltpu.make_async_copy(vm, hb, sem))
        return copies

    def move(hbm, vmem_ref, seq, load):
        copies = slab_copies(hbm, vmem_ref, seq, load)
        for cp in copies:
            cp.start()
        for cp in copies:
            cp.wait()

    def gather_row(c, src_step, xin_ref):
        t = idx_ref[src_step * cap + c]
        xin_ref[pl.ds(c, n_slab, stride=cap_pitch), :] = h2_ref[pl.ds(t, n_slab, stride=seq_pitch), :]

    def scatter_rows(first, src_step, live):
        updates = []
        for i in range(SCATTER_BATCH):
            t = idx_ref[src_step * cap + first + i]
            g = gate_ref[src_step * cap + first + i] * live
            w = y_ref[pl.ds(first + i, n_slab, stride=cap_pitch), :] * g
            updates.append((t, acc_ref[pl.ds(t, n_slab, stride=seq_pitch), :] + w))
        for t, v in updates:
            acc_ref[pl.ds(t, n_slab, stride=seq_pitch), :] = v

    @pl.when(step == 0)
    def _():
        y_ref[...] = jnp.zeros_like(y_ref)
        loads = slab_copies(x1_hbm, acc_ref, b, True) + slab_copies(h2_hbm, h2_ref, b, True, sem0=n_slab)
        for cp in loads:
            cp.start()
        for cp in loads:
            cp.wait()

        def fill(c, carry):
            gather_row(c, step, xin_even_ref)
            return carry

        lax.fori_loop(0, cap, fill, 0)

    @pl.when((e == n_e - 1) & (b + 1 < n_b))
    def _():
        move(h2_hbm, h2_ref, b + 1, True)

    def pipeline_step(xin_ref, xin_next_ref):
        xb = jnp.concatenate([xin_ref[k * cap_pitch:k * cap_pitch + cap, :] for k in range(n_slab)],
                             axis=1).astype(BF16)
        nxt = jnp.minimum(step + 1, last)
        for c in range(cap):
            gather_row(c, nxt, xin_next_ref)
        a = jnp.dot(xb, wg_ref[...], preferred_element_type=F32)
        u = jnp.dot(xb, wu_ref[...], preferred_element_type=F32)
        hm = (a * (1.0 / (1.0 + jnp.exp(-a))) * u).astype(BF16)
        y = jnp.dot(hm, wd_ref[...], preferred_element_type=F32)
        prev = jnp.maximum(step - 1, 0)
        live = (step > 0).astype(F32)
        for first in range(0, cap, SCATTER_BATCH):
            scatter_rows(first, prev, live)
        for k in range(n_slab):
            y_ref[k * cap_pitch:k * cap_pitch + cap, :] = y[:, k * LANES:(k + 1) * LANES]

    @pl.when(step % 2 == 0)
    def _():
        pipeline_step(xin_even_ref, xin_odd_ref)

    @pl.when(step % 2 == 1)
    def _():
        pipeline_step(xin_odd_ref, xin_even_ref)

    @pl.when((e == 0) & (b > 0))
    def _():
        stores = slab_copies(out_hbm, acc_ref, b - 1, False)
        loads = slab_copies(x1_hbm, acc_ref, b, True, sem0=n_slab)
        for cp in stores:
            cp.start()
        for st, ld in zip(stores, loads):
            st.wait()
            ld.start()
        for cp in loads:
            cp.wait()

    @pl.when(step == last)
    def _():
        def drain(j, carry):
            scatter_rows(j * SCATTER_BATCH, step, jnp.float32(1.0))
            return carry

        lax.fori_loop(0, cap // SCATTER_BATCH, drain, 0)
        move(out_hbm, acc_ref, b, False)


def _moe_call(idx_flat, gate_flat, x1, h2, wg, wu, wd, cap):
    B, S, D = x1.shape
    E, _, FF = wg.shape
    assert D % LANES == 0 and cap % SCATTER_BATCH == 0
    n_slab = D // LANES
    smem = pl.BlockSpec(memory_space=pltpu.SMEM)
    hbm = pl.BlockSpec(memory_space=pl.ANY)
    return pl.pallas_call(
        functools.partial(_moe_kernel, cap, S, D),
        grid=(B, E),
        in_specs=[smem, smem, hbm, hbm,
                  pl.BlockSpec((None, D, FF), lambda b, e: (e, 0, 0)),
                  pl.BlockSpec((None, D, FF), lambda b, e: (e, 0, 0)),
                  pl.BlockSpec((None, FF, D), lambda b, e: (e, 0, 0))],
        out_specs=hbm,
        out_shape=jax.ShapeDtypeStruct((B, S, D), F32),
        scratch_shapes=[pltpu.VMEM((n_slab * (S + SLAB_PAD), LANES), F32),
                        pltpu.VMEM((n_slab * (S + SLAB_PAD), LANES), F32),
                        pltpu.VMEM((n_slab * (cap + SLAB_PAD), LANES), F32),
                        pltpu.VMEM((n_slab * (cap + SLAB_PAD), LANES), F32),
                        pltpu.VMEM((n_slab * (cap + SLAB_PAD), LANES), F32),
                        pltpu.SemaphoreType.DMA((2 * n_slab,))],
        compiler_params=_cparams(("arbitrary", "arbitrary")),
        name="moe",
    )(idx_flat, gate_flat, x1, h2, wg, wu, wd)


def _rope_tables(S):
    half = HEAD_DIM // 2
    inv = ROPE_THETA ** (-jnp.arange(half, dtype=F32) / half)
    ang = jnp.arange(S, dtype=F32)[:, None] * inv[None, :]
    cos = jnp.cos(ang)
    sin = jnp.sin(ang)
    heads = LANE_CHUNK // HEAD_DIM
    return (jnp.tile(cos, (1, 2 * heads)),
            jnp.concatenate([jnp.tile(-sin, (1, heads)), jnp.tile(sin, (1, heads))], axis=-1))


def _split_half_columns(w, start, width):
    rows = w.shape[0]
    seg = w[:, start:start + width].reshape(rows, width // QUAD, 4, 2, HEAD_DIM // 2)
    seg = seg.transpose(0, 1, 3, 2, 4).reshape(rows, width)
    return jnp.concatenate([w[:, :start], seg, w[:, start + width:]], axis=1)


def _layer(x, norm1_g, w_in, dil_q_g, dil_k_g, na_q_g, na_k_g, na_rpb, w_dil_branch, w_na_branch,
           w_out, norm2_g, w_router, w_gate, w_up, w_down):
    B, S, D = x.shape
    M = B * S
    tm = 512
    scale = HEAD_DIM ** -0.5
    dil_heads = DIL_WIDTH // HEAD_DIM
    half = HEAD_DIM // 2

    def quad_gains(g, n_heads):
        quad = jnp.concatenate([jnp.tile(g[:half], 4), jnp.tile(g[half:], 4)])
        return jnp.tile(quad, n_heads // 4)

    gain_cols = jnp.concatenate([
        quad_gains(dil_q_g * scale, dil_heads), quad_gains(dil_k_g, dil_heads), jnp.ones((DIL_WIDTH,), F32),
        quad_gains(na_q_g * scale, NA_HEADS), quad_gains(na_k_g, NA_HEADS), jnp.ones((NA_WIDTH + 2 * D,), F32),
    ])[None, :].astype(F32)
    cos_t, sin_t = _rope_tables(S)
    head_of_lane = _split_half_order() // HEAD_DIM
    bd = jnp.asarray(head_of_lane[:, None] == head_of_lane[None, :], BF16)
    w_cols = _split_half_columns(w_in, 0, 2 * DIL_WIDTH)
    w_cols = _split_half_columns(w_cols, 3 * DIL_WIDTH, 2 * NA_WIDTH)

    x2 = x.reshape(M, D)
    proj = _proj_call(x2, norm1_g[None, :], w_cols.astype(BF16), gain_cols, cos_t, sin_t, bd, S, D, tm)
    n_groups = len(DIL_PAIRS)
    qa, ka, va = proj[:n_groups], proj[n_groups:2 * n_groups], proj[2 * n_groups:3 * n_groups]
    qn, kn, vn, ga, gn = proj[3 * n_groups:]

    o_parts, st_parts = [], []
    for g, (window, dilation) in enumerate(DIL_PAIRS):
        o, st = _dilated_call(qa[g], ka[g], va[g], B, S, g, window, dilation)
        o_parts.append(o)
        st_parts.append(st)

    yb = _na_call(qn.reshape(B, S, -1), kn.reshape(B, S, -1), vn.reshape(B, S, -1),
                  _na_bias_table(na_rpb)).reshape(M, NA_WIDTH)

    wr_t = w_router.T
    wr_hi = wr_t.astype(BF16)
    wr_lo = (wr_t - wr_hi.astype(F32)).astype(BF16)
    x1, h2, aff = _mix_call(x2, o_parts, st_parts, yb, ga, gn,
                            w_dil_branch.astype(BF16), w_na_branch.astype(BF16), w_out.astype(BF16),
                            norm2_g[None, :], wr_hi, wr_lo, B, S, MIX_TILE_ROWS)

    cap = (EC_CAPACITY_FACTOR * S) // N_EXPERTS
    idx, gates = _select_call(aff, N_EXPERTS, cap)
    return _moe_call(idx.reshape(-1), gates.reshape(-1),
                     x1.reshape(B, S, D), h2.reshape(B, S, D),
                     w_gate.astype(BF16), w_up.astype(BF16), w_down.astype(BF16), cap)


def kernel(x, norm1_g, w_in, dil_q_norm_g, dil_k_norm_g, na_q_norm_g, na_k_norm_g, na_rpb, w_dil_branch,
           w_na_branch, w_out, norm2_g, w_router, w_gate, w_up, w_down):
    for l in range(norm1_g.shape[0]):
        x = _layer(x, norm1_g[l], w_in[l], dil_q_norm_g[l], dil_k_norm_g[l], na_q_norm_g[l],
                   na_k_norm_g[l], na_rpb[l], w_dil_branch[l], w_na_branch[l], w_out[l], norm2_g[l],
                   w_router[l], w_gate[l], w_up[l], w_down[l])
    return x
```

```python
import functools
import math

import jax
import jax.numpy as jnp
import numpy as np
from jax import lax
from jax.experimental import pallas as pl
from jax.experimental.pallas import tpu as pltpu

F32 = jnp.float32
BF16 = jnp.bfloat16

HEAD_DIM = 64
DIL_PAIRS = ((128, 1), (512, 4), (2048, 16))
DIL_GROUP_HEADS = 4
DIL_WIDTH = DIL_GROUP_HEADS * len(DIL_PAIRS) * HEAD_DIM
DIL_OUT = DIL_GROUP_HEADS * HEAD_DIM
NA_HEADS = 8
NA_WIDTH = NA_HEADS * HEAD_DIM
GRID_W = 64
NA_KH = 8
NA_KW = 16
N_EXPERTS = 16
EC_CAPACITY_FACTOR = 2
ROPE_THETA = 10000.0
EPS = 1e-6
MASKED = -1e30

LANE_CHUNK = 256
QUAD = 4 * HEAD_DIM
STAT_LANES = 128
LANES = 128
SLAB_PAD = 8
SCATTER_BATCH = 4
DIL_RESIDUES_PER_STEP = 16
MIX_TILE_ROWS = 1024
MIX_SUB_ROWS = 512
VMEM_LIMIT = 56 * 1024 * 1024


def _cparams(sem):
    return pltpu.CompilerParams(dimension_semantics=sem, vmem_limit_bytes=VMEM_LIMIT)


def _const_spec(shape):
    nd = len(shape)
    return pl.BlockSpec(shape, lambda *_: (0,) * nd, pipeline_mode=pl.Buffered(1))


def _proj_kernel(segs, x_ref, g1_ref, w_ref, gain_ref, cos_ref, sin_ref, bd_ref, *out_and_scratch):
    *out_refs, tmp_ref = out_and_scratch
    x = x_ref[...]
    ms = jnp.mean(x * x, axis=-1, keepdims=True)
    h = (x * lax.rsqrt(ms + EPS) * g1_ref[...]).astype(BF16)
    bd = bd_ref[...]

    def rope(y):
        partner = jnp.concatenate([y[:, LANES:], y[:, :LANES]], axis=1)
        return y * cos_ref[...] + partner * sin_ref[...]

    tm = x.shape[0]
    chunks, col = [], 0
    for ref, (width, kind, dil) in zip(out_refs, segs):
        for c in range(0, width, LANE_CHUNK):
            chunks.append((ref, c, kind, dil, col + c))
        col += width
    raw = {}
    for norm_kind in ("qk_rope", "qk"):
        normed = [ch for ch in chunks if ch[2] == norm_kind]
        for ch in normed:
            raw[ch[4]] = jnp.dot(h, w_ref[:, ch[4]:ch[4] + LANE_CHUNK], preferred_element_type=F32)
        squares = jnp.concatenate([(raw[ch[4]] * raw[ch[4]]).astype(BF16) for ch in normed], axis=0)
        sumsq = jnp.dot(squares, bd, preferred_element_type=F32)
        for i, ch in enumerate(normed):
            ss = sumsq[i * tm:(i + 1) * tm]
            raw[ch[4]] = (raw[ch[4]] * lax.rsqrt(ss * (1.0 / HEAD_DIM) + EPS)
                          * gain_ref[:, ch[4]:ch[4] + LANE_CHUNK])

    for ref, c, kind, dil, c_abs in chunks:
        if kind == "qk_rope":
            p = rope(raw[c_abs])
        elif kind == "qk":
            p = raw[c_abs]
        else:
            p = jnp.dot(h, w_ref[:, c_abs:c_abs + LANE_CHUNK], preferred_element_type=F32)
        if kind == "gate":
            p = 1.0 / (1.0 + jnp.exp(-p))
        if dil == 1:
            ref[:, c:c + LANE_CHUNK] = p.astype(ref.dtype)
        else:
            for half in range(LANE_CHUNK // LANES):
                tmp_ref[half] = p[:, half * LANES:(half + 1) * LANES]
            for r in range(dil):
                for half in range(LANE_CHUNK // LANES):
                    c0 = r * LANE_CHUNK + half * LANES
                    ref[:, c0:c0 + LANES] = tmp_ref[half, pl.ds(r, tm // dil, stride=dil), :].astype(ref.dtype)


def _proj_call(x2, g1, w_in, gain_cols, cos_t, sin_t, bd, S, d_model, tm):
    M = x2.shape[0]
    dils = [d for _, d in DIL_PAIRS]
    segs = tuple((QUAD, kind, d) for kind in ("qk_rope", "qk_rope", "v") for d in dils) + (
        (NA_WIDTH, "qk", 1), (NA_WIDTH, "qk", 1), (NA_WIDTH, "v", 1),
        (d_model, "gate", 1), (d_model, "gate", 1))
    n_cols = sum(w for w, _, _ in segs)
    pos_blocks = S // tm
    row = lambda i: (i, 0)

    return pl.pallas_call(
        functools.partial(_proj_kernel, segs),
        grid=(M // tm,),
        in_specs=[
            pl.BlockSpec((tm, d_model), row),
            _const_spec((1, d_model)),
            _const_spec((d_model, n_cols)),
            _const_spec((1, n_cols)),
            pl.BlockSpec((tm, LANE_CHUNK), lambda i: (i % pos_blocks, 0)),
            pl.BlockSpec((tm, LANE_CHUNK), lambda i: (i % pos_blocks, 0)),
            _const_spec((LANE_CHUNK, LANE_CHUNK)),
        ],
        out_specs=[pl.BlockSpec((tm // d, w * d), row) for w, _, d in segs],
        out_shape=[jax.ShapeDtypeStruct((M // d, w * d), BF16) for w, _, d in segs],
        scratch_shapes=[pltpu.VMEM((LANE_CHUNK // LANES, tm, LANES), F32)],
        compiler_params=_cparams(("parallel",)),
        name="proj",
    )(x2, g1, w_in, gain_cols, cos_t, sin_t, bd)


def _split_half_order():
    n = np.arange(QUAD)
    half, head, i = n // LANES, (n % LANES) // (HEAD_DIM // 2), n % (HEAD_DIM // 2)
    return head * HEAD_DIM + half * (HEAD_DIM // 2) + i


def _stack_heads(q):
    lane = lax.broadcasted_iota(jnp.int32, (1, QUAD), 1)
    q_head = (lane % LANES) // (HEAD_DIM // 2)
    zero = jnp.zeros_like(q)
    return jnp.concatenate([jnp.where(q_head == h, q, zero) for h in range(4)], axis=0)


def _pick_heads(parts, lane_head):
    out = parts[3]
    for h in (2, 1, 0):
        out = jnp.where(lane_head == h, parts[h], out)
    return out


def _dil_kernel(L, T, R, n_res, q_ref, k_ref, v_ref, o_ref, st_ref):
    KW = T + 2 * R
    lane_head = lax.broadcasted_iota(jnp.int32, (1, QUAD), 1) // HEAD_DIM
    stat_lane = lax.broadcasted_iota(jnp.int32, (1, STAT_LANES), 1)
    qi = lax.broadcasted_iota(jnp.int32, (T, KW), 0)
    kj = lax.broadcasted_iota(jnp.int32, (T, KW), 1)

    def body(i, carry):
        q0 = pl.multiple_of(i * T, T)
        ks = pl.multiple_of(jnp.clip(q0 - R, 0, L - KW), R)
        off = kj - qi + (ks - q0)
        bias = jnp.where(jnp.abs(off) <= R, 0.0, MASKED).astype(F32)
        bias = jnp.concatenate([bias] * 4, axis=0)
        for res in range(n_res):
            block(q0, ks, bias, slice(res * QUAD, (res + 1) * QUAD),
                  slice(res * STAT_LANES, (res + 1) * STAT_LANES))
        return carry

    def block(q0, ks, bias, lanes, stat_lanes):
        q = q_ref[pl.ds(q0, T), lanes]
        k = k_ref[pl.ds(ks, KW), lanes]
        v = v_ref[pl.ds(ks, KW), lanes]
        s = lax.dot_general(_stack_heads(q), k,(((1,), (1,)), ((), ())),
                            preferred_element_type=F32)
        s = s + bias
        m = jnp.max(s, axis=-1, keepdims=True)
        p = jnp.exp(s - m)
        den = jnp.sum(p, axis=-1, keepdims=True)
        o = jnp.dot(p.astype(BF16), v, preferred_element_type=F32) / den
        stat = jnp.zeros((T, STAT_LANES), F32)
        for h in range(4):
            stat = jnp.where(stat_lane == h, m[h * T:(h + 1) * T], stat)
            stat = jnp.where(stat_lane == 4 + h, den[h * T:(h + 1) * T], stat)
        outs = [o[h * T:(h + 1) * T] for h in range(4)]
        o_ref[pl.ds(q0, T), lanes] = _pick_heads(outs, lane_head).astype(o_ref.dtype)
        st_ref[pl.ds(q0, T), stat_lanes] = stat

    lax.fori_loop(0, L // T, body, 0, unroll=max(1, min(L // T, 32 // n_res)))


def _dilated_call(q, k, v, B, S, group, window, dilation, T=128):
    d = dilation
    R = (window // 2) // d
    L = S // d
    assert L % T == 0 and L >= T + 2 * R and R % 16 == 0
    view = lambda t: t.reshape(B, L, d * QUAD)
    n_res = min(d, DIL_RESIDUES_PER_STEP)
    in_spec = pl.BlockSpec((None, L, n_res * QUAD), lambda b, r: (b, 0, r))
    o, st = pl.pallas_call(
        functools.partial(_dil_kernel, L, T, R, n_res),
        grid=(B, d // n_res),
        in_specs=[in_spec, in_spec, in_spec],
        out_specs=[pl.BlockSpec((None, L, n_res * QUAD), lambda b, r: (b, 0, r)),
                   pl.BlockSpec((None, L, n_res * STAT_LANES), lambda b, r: (b, 0, r))],
        out_shape=[jax.ShapeDtypeStruct((B, L, d * QUAD), BF16),
                   jax.ShapeDtypeStruct((B, L, d * STAT_LANES), F32)],
        compiler_params=_cparams(("parallel", "parallel")),
        name=f"dilated_g{group}",
    )(view(q), view(k), view(v))
    return o.reshape(B * L, d * QUAD), st.reshape(B * L, d * STAT_LANES)


def _na_kernel(rows, RB, q_ref, k_ref, v_ref, bias_ref, o_ref):
    lane_head = lax.broadcasted_iota(jnp.int32, (1, QUAD), 1) // HEAD_DIM
    n_keys = NA_KH * GRID_W
    i = pl.program_id(1)
    for rr in range(RB):
        r = i * RB + rr
        rs = jnp.clip(r - NA_KH // 2, 0, rows - NA_KH)
        e = rs - r + NA_KH - 1
        k0 = pl.multiple_of(rs * GRID_W, GRID_W)
        for quad in range(NA_WIDTH // QUAD):
            lanes = slice(quad * QUAD, (quad + 1) * QUAD)
            q = q_ref[rr * GRID_W:(rr + 1) * GRID_W, lanes]
            k = k_ref[pl.ds(k0, n_keys), lanes]
            v = v_ref[pl.ds(k0, n_keys), lanes]
            s = lax.dot_general(_stack_heads(q), k,(((1,), (1,)), ((), ())),
                                preferred_element_type=F32)
            s = s + bias_ref[quad, e]
            m = jnp.max(s, axis=-1, keepdims=True)
            p = jnp.exp(s - m)
            den = jnp.sum(p, axis=-1, keepdims=True)
            o = jnp.dot(p.astype(BF16), v, preferred_element_type=F32) / den
            outs = [o[h * GRID_W:(h + 1) * GRID_W] for h in range(4)]
            o_ref[rr * GRID_W:(rr + 1) * GRID_W, lanes] = _pick_heads(outs, lane_head).astype(o_ref.dtype)


def _na_bias_table(rpb):
    c = np.arange(GRID_W)
    ws = np.clip(c - NA_KW // 2, 0, GRID_W - NA_KW)
    col_ok = (c[None, :] >= ws[:, None]) & (c[None, :] < ws[:, None] + NA_KW)
    dc = np.clip(c[None, :] - c[:, None] + NA_KW - 1, 0, 2 * NA_KW - 2)
    n_dc = 2 * NA_KW - 1
    n_heads, n_dr = rpb.shape[0], rpb.shape[1]
    onehot = np.zeros((n_dc, GRID_W * GRID_W), np.float32)
    onehot[dc.reshape(-1), np.arange(GRID_W * GRID_W)] = 1.0
    a = jnp.dot(rpb.reshape(n_heads * n_dr, n_dc).astype(F32), onehot, precision=lax.Precision.HIGHEST)
    a = jnp.where(col_ok[None, None], a.reshape(n_heads, n_dr, GRID_W, GRID_W), MASKED)
    t = jnp.stack([a[:, e:e + NA_KH] for e in range(NA_KH)], axis=1)
    t = t.reshape(n_heads // 4, 4, NA_KH, NA_KH, GRID_W, GRID_W).transpose(0, 2, 1, 4, 3, 5)
    return t.reshape(n_heads // 4, NA_KH, 4 * GRID_W, NA_KH * GRID_W)


def _na_call(qn, kn, vn, bias, RB=16):
    B, S, _ = qn.shape
    rows = S // GRID_W
    assert rows >= NA_KH and rows % RB == 0
    blk = pl.BlockSpec((None, RB * GRID_W, NA_WIDTH), lambda b, i: (b, i, 0))
    full = pl.BlockSpec((None, S, NA_WIDTH), lambda b, i: (b, 0, 0))
    return pl.pallas_call(
        functools.partial(_na_kernel, rows, RB),
        grid=(B, rows // RB),
        in_specs=[blk, full, full, _const_spec(bias.shape)],
        out_specs=blk,
        out_shape=jax.ShapeDtypeStruct((B, S, NA_WIDTH), BF16),
        compiler_params=_cparams(("parallel", "arbitrary")),
        name="neighbourhood",
    )(qn, kn, vn, bias)


def _token_major(o_ref, s_ref, o_tmp, s_tmp, dil):
    if dil == 1:
        return o_ref[...].astype(F32), s_ref[...]
    rows = o_ref.shape[0]
    halves = QUAD // LANES
    for r in range(dil):
        blk = o_ref[:, r * QUAD:(r + 1) * QUAD].astype(F32)
        for half in range(halves):
            o_tmp[half, pl.ds(r, rows, stride=dil), :] = blk[:, half * LANES:(half + 1) * LANES]
        s_tmp[pl.ds(r, rows, stride=dil), :] = s_ref[:, r * STAT_LANES:(r + 1) * STAT_LANES]
    return jnp.concatenate([o_tmp[half] for half in range(halves)], axis=1), s_tmp[...]


def _mix_kernel(dils, x_ref, o0_ref, o1_ref, o2_ref, s0_ref, s1_ref, s2_ref, yb_ref, ga_ref, gn_ref,
                pa_ref, pb_ref, wo_ref, g2_ref, wrh_ref, wrl_ref, x1_ref, h2_ref, aff_ref, o_tmp, s_tmp):
    tile_outs, tile_stats = [], []
    for g, (o_ref, s_ref, dil) in enumerate(zip((o0_ref, o1_ref, o2_ref), (s0_ref, s1_ref, s2_ref), dils)):
        o, st = _token_major(o_ref, s_ref, o_tmp.at[g], s_tmp.at[g], dil)
        tile_outs.append(o)
        tile_stats.append(st)
    for r0 in range(0, x_ref.shape[0], MIX_SUB_ROWS):
        rows = slice(r0, r0 + MIX_SUB_ROWS)
        _mix_rows([o[rows] for o in tile_outs], [st[rows] for st in tile_stats],
                  x_ref.at[rows], yb_ref.at[rows], ga_ref.at[rows], gn_ref.at[rows],
                  pa_ref, pb_ref, wo_ref, g2_ref, wrh_ref, wrl_ref,
                  x1_ref.at[rows], h2_ref.at[rows], aff_ref, r0 // LANES)


def _mix_rows(outs, stats, x_ref, yb_ref, ga_ref, gn_ref, pa_ref, pb_ref, wo_ref, g2_ref, wrh_ref, wrl_ref,
              x1_ref, h2_ref, aff_ref, first_group):
    lane_head = lax.broadcasted_iota(jnp.int32, (1, QUAD), 1) // HEAD_DIM
    coef = [None, None, None]
    for h in range(4):
        ms = [st[:, h:h + 1] for st in stats]
        dens = [st[:, 4 + h:5 + h] for st in stats]
        top = jnp.maximum(jnp.maximum(ms[0], ms[1]), ms[2])
        ws = [dens[g] * jnp.exp(ms[g] - top) for g in range(3)]
        tot = ws[0] + ws[1] + ws[2]
        for g in range(3):
            cg = ws[g] / tot
            coef[g] = cg if coef[g] is None else jnp.where(lane_head == h, cg, coef[g])
    ya = coef[0] * outs[0] + coef[1] * outs[1] + coef[2] * outs[2]
    ta = jnp.dot(ya.astype(BF16), pa_ref[...], preferred_element_type=F32)
    tb = jnp.dot(yb_ref[...], pb_ref[...], preferred_element_type=F32)
    merged = ga_ref[...].astype(F32) * ta + gn_ref[...].astype(F32) * tb
    x1 = x_ref[...] + jnp.dot(merged.astype(BF16), wo_ref[...], preferred_element_type=F32)
    x1_ref[...] = x1
    ms2 = jnp.mean(x1 * x1, axis=-1, keepdims=True)
    h2 = x1 * lax.rsqrt(ms2 + EPS) * g2_ref[...]
    h2_ref[...] = h2
    h_hi = h2.astype(BF16)
    h_lo = (h2 - h_hi.astype(F32)).astype(BF16)
    nt = (((1,), (1,)), ((), ()))
    both = lax.dot_general(jnp.concatenate([wrh_ref[...], wrl_ref[...]], axis=0), h_hi, nt,
                           preferred_element_type=F32)
    n_exp = wrh_ref.shape[0]
    logits = (both[:n_exp] + lax.dot_general(wrh_ref[...], h_lo, nt, preferred_element_type=F32)
              + both[n_exp:])
    z = jnp.exp(logits - jnp.max(logits, axis=0, keepdims=True))
    aff = z / jnp.sum(z, axis=0, keepdims=True)
    groups = aff_ref.shape[0] // n_exp
    for j in range(aff.shape[1] // LANES):
        aff_ref[pl.ds(first_group + j, n_exp, stride=groups), :] = aff[:, j * LANES:(j + 1) * LANES]


def _mix_call(x2, o_parts, st_parts, yb, ga, gn, pa, pb, wo, g2, wr_hi, wr_lo, B, S, tm):
    M, D = x2.shape
    row = lambda i: (i, 0)
    blocks = S // tm
    rows_of = lambda w, d=1: pl.BlockSpec((tm // d, w * d), row)
    dils = tuple(d for _, d in DIL_PAIRS)
    return pl.pallas_call(
        functools.partial(_mix_kernel, dils),
        grid=(M // tm,),
        in_specs=[rows_of(D)] + [rows_of(QUAD, d) for d in dils] + [rows_of(STAT_LANES, d) for d in dils]
                 + [rows_of(NA_WIDTH), rows_of(D), rows_of(D),
                    _const_spec(pa.shape), _const_spec(pb.shape), _const_spec(wo.shape),
                    _const_spec(g2.shape), _const_spec(wr_hi.shape), _const_spec(wr_lo.shape)],
        out_specs=[rows_of(D), rows_of(D),
                   pl.BlockSpec((None, None, N_EXPERTS * (tm // LANES), LANES),
                                lambda i: (i // blocks, i % blocks, 0, 0))],
        out_shape=[jax.ShapeDtypeStruct((M, D), F32), jax.ShapeDtypeStruct((M, D), F32),
                   jax.ShapeDtypeStruct((B, blocks, N_EXPERTS * (tm // LANES), LANES), F32)],
        scratch_shapes=[pltpu.VMEM((len(dils), QUAD // LANES, tm, LANES), F32),
                        pltpu.VMEM((len(dils), tm, LANES), F32)],
        compiler_params=_cparams(("parallel",)),
        name="mix",
    )(x2, *o_parts, *st_parts, yb, ga, gn, pa, pb, wo, g2, wr_hi, wr_lo)


def _select_kernel(cap, n_exp, tiles, groups, aff_ref, tri_ref, below_ref, idx_ref, gate_ref):
    n_bits = 31
    rows = tiles * groups
    affs = [jnp.concatenate([aff_ref[(t * n_exp + e) * groups:(t * n_exp + e + 1) * groups, :]
                             for t in range(tiles)], axis=0) for e in range(n_exp)]
    as_float = lambda word: pltpu.bitcast(word, F32)

    def count(mask):
        return jnp.sum(jnp.sum(mask.astype(F32), axis=1, keepdims=True), axis=0, keepdims=True)

    def refine(i, thrs):
        bit = lax.shift_left(jnp.int32(1), n_bits - 1 - i)
        out = []
        for e in range(n_exp):
            cand = thrs[e] | bit
            out.append(jnp.where(count(affs[e] >= as_float(cand)) >= cap, cand, thrs[e]))
        return tuple(out)

    thrs = lax.fori_loop(0, n_bits, refine, tuple(jnp.zeros((1, 1), jnp.int32) for _ in range(n_exp)))

    tri = tri_ref[...]
    below = below_ref[...]

    def running_count(mask):
        in_row = jnp.dot(mask.astype(BF16), tri, preferred_element_type=F32)
        row_tot = jnp.broadcast_to(in_row[:, LANES - 1:LANES], in_row.shape).astype(BF16)
        return in_row + jnp.dot(below, row_tot, preferred_element_type=F32)

    cuts = [as_float(t) for t in thrs]
    above = jnp.concatenate([affs[e] > cuts[e] for e in range(n_exp)], axis=0)
    equal = jnp.concatenate([affs[e] == cuts[e] for e in range(n_exp)], axis=0)
    room = jnp.concatenate(
        [jnp.broadcast_to(cap - count(affs[e] > cuts[e]), (rows, LANES)) for e in range(n_exp)], axis=0)
    chosen = above | (equal & (running_count(equal) <= room))
    cnt = running_count(chosen)

    lane = lax.broadcasted_iota(jnp.int32, (1, LANES), 1)
    lane_f = lane.astype(F32)
    slot = lax.broadcasted_iota(jnp.int32, (cap, 1), 0).astype(F32)
    diag = lax.broadcasted_iota(jnp.int32, (rows, LANES), 0) == lax.broadcasted_iota(jnp.int32, (rows, LANES), 1)
    pad = jnp.zeros((LANES - rows, LANES), BF16)
    idx_out = jnp.zeros((cap, LANES), jnp.int32)
    gate_out = jnp.zeros((cap, LANES), F32)
    for e in range(n_exp):
        cnt_e = cnt[e * rows:(e + 1) * rows]
        aff_e = affs[e]
        row_end = jnp.sum(jnp.where(diag, cnt_e[:, LANES - 1:LANES], 0.0), axis=0, keepdims=True)
        row_end = jnp.where(lane < rows, row_end, float(2 * cap))
        row_of = jnp.sum((row_end <= slot).astype(F32), axis=1, keepdims=True)
        pick = (lane_f == row_of).astype(BF16)
        cnt_hi = jnp.floor(cnt_e * (1.0 / 256.0))
        a_hi = aff_e.astype(BF16)
        a_r1 = aff_e - a_hi.astype(F32)
        a_mid = a_r1.astype(BF16)
        a_lo = (a_r1 - a_mid.astype(F32)).astype(BF16)
        pieces = [cnt_hi.astype(BF16), (cnt_e - 256.0 * cnt_hi).astype(BF16), a_hi, a_mid, a_lo]
        rhs = jnp.concatenate([jnp.concatenate([p, pad], axis=0) for p in pieces], axis=1)
        got = jnp.dot(pick, rhs, preferred_element_type=F32)
        row_cnt = 256.0 * got[:, 0:LANES] + got[:, LANES:2 * LANES]
        row_aff = got[:, 2 * LANES:3 * LANES] + got[:, 3 * LANES:4 * LANES] + got[:, 4 * LANES:5 * LANES]
        lane_of = jnp.sum((row_cnt <= slot).astype(F32), axis=1, keepdims=True)
        gate = jnp.sum(jnp.where(lane_f == lane_of, row_aff, 0.0), axis=1, keepdims=True)
        token = (row_of * float(LANES) + lane_of).astype(jnp.int32)
        idx_out = jnp.where(lane == e, token, idx_out)
        gate_out = jnp.where(lane == e, gate, gate_out)
    idx_ref[...] = idx_out
    gate_ref[...] = gate_out


def _select_call(aff, E, cap):
    B, tiles, eg, _ = aff.shape
    groups = eg // E
    rows = tiles * groups
    assert groups % 8 == 0 and rows <= LANES and cap <= 256 * 256
    r = np.arange(E * rows)
    tri = jnp.asarray(np.arange(LANES)[:, None] <= np.arange(LANES)[None, :], BF16)
    below = jnp.asarray((r[:, None] // rows == r[None, :] // rows) & (r[None, :] < r[:, None]), BF16)
    out_spec = pl.BlockSpec((None, cap, LANES), lambda b: (b, 0, 0))
    idx, gate = pl.pallas_call(
        functools.partial(_select_kernel, cap, E, tiles, groups),
        grid=(B,),
        in_specs=[pl.BlockSpec((None, E * rows, LANES), lambda b: (b, 0, 0)),
                  _const_spec(tri.shape), _const_spec(below.shape)],
        out_specs=[out_spec, out_spec],
        out_shape=[jax.ShapeDtypeStruct((B, cap, LANES), jnp.int32),
                   jax.ShapeDtypeStruct((B, cap, LANES), F32)],
        compiler_params=_cparams(("parallel",)),
        name="select",
    )(aff.reshape(B, E * rows, LANES), tri, below)
    unpack = lambda t: t[:, :, :E].transpose(0, 2, 1)
    return unpack(idx), unpack(gate)


def _moe_kernel(cap, S, D, idx_ref, gate_ref, x1_hbm, h2_hbm, wg_ref, wu_ref, wd_ref, out_hbm,
                acc_ref, h2_ref, xin_even_ref, xin_odd_ref, y_ref, slab_sem):
    n_slab = D // LANES
    seq_pitch = S + SLAB_PAD
    cap_pitch = cap + SLAB_PAD
    b = pl.program_id(0)
    e = pl.program_id(1)
    n_b = pl.num_programs(0)
    n_e = pl.num_programs(1)
    step = b * n_e + e
    last = n_b * n_e - 1

    def slab_copies(hbm, vmem_ref, seq, load, sem0=0):
        copies = []
        for k in range(n_slab):
            vm = vmem_ref.at[pl.ds(k * seq_pitch, S), :]
            hb = hbm.at[seq, :, pl.ds(k * LANES, LANES)]
            sem = slab_sem.at[sem0 + k]
            copies.append(pltpu.make_async_copy(hb, vm, sem) if load else pltpu.make_async_copy(vm, hb, sem))
        return copies

    def move(hbm, vmem_ref, seq, load):
        copies = slab_copies(hbm, vmem_ref, seq, load)
        for cp in copies:
            cp.start()
        for cp in copies:
            cp.wait()

    def gather_row(c, src_step, xin_ref):
        t = idx_ref[src_step * cap + c]
        xin_ref[pl.ds(c, n_slab, stride=cap_pitch), :] = h2_ref[pl.ds(t, n_slab, stride=seq_pitch), :]

    def scatter_rows(first, src_step, live):
        updates = []
        for i in range(SCATTER_BATCH):
            t = idx_ref[src_step * cap + first + i]
            g = gate_ref[src_step * cap + first + i] * live
            w = y_ref[pl.ds(first + i, n_slab, stride=cap_pitch), :] * g
            updates.append((t, acc_ref[pl.ds(t, n_slab, stride=seq_pitch), :] + w))
        for t, v in updates:
            acc_ref[pl.ds(t, n_slab, stride=seq_pitch), :] = v

    @pl.when(step == 0)
    def _():
        y_ref[...] = jnp.zeros_like(y_ref)
        loads = slab_copies(x1_hbm, acc_ref, b, True) + slab_copies(h2_hbm, h2_ref, b, True, sem0=n_slab)
        for cp in loads:
            cp.start()
        for cp in loads:
            cp.wait()

        def fill(c, carry):
            gather_row(c, step, xin_even_ref)
            return carry

        lax.fori_loop(0, cap, fill, 0)

    @pl.when((e == n_e - 1) & (b + 1 < n_b))
    def _():
        move(h2_hbm, h2_ref, b + 1, True)

    def pipeline_step(xin_ref, xin_next_ref):
        xb = jnp.concatenate([xin_ref[k * cap_pitch:k * cap_pitch + cap, :] for k in range(n_slab)],
                             axis=1).astype(BF16)
        nxt = jnp.minimum(step + 1, last)
        for c in range(cap):
            gather_row(c, nxt, xin_next_ref)
        a = jnp.dot(xb, wg_ref[...], preferred_element_type=F32)
        u = jnp.dot(xb, wu_ref[...], preferred_element_type=F32)
        hm = (a * (1.0 / (1.0 + jnp.exp(-a))) * u).astype(BF16)
        y = jnp.dot(hm, wd_ref[...], preferred_element_type=F32)
        prev = jnp.maximum(step - 1, 0)
        live = (step > 0).astype(F32)
        for first in range(0, cap, SCATTER_BATCH):
            scatter_rows(first, prev, live)
        for k in range(n_slab):
            y_ref[k * cap_pitch:k * cap_pitch + cap, :] = y[:, k * LANES:(k + 1) * LANES]

    @pl.when(step % 2 == 0)
    def _():
        pipeline_step(xin_even_ref, xin_odd_ref)

    @pl.when(step % 2 == 1)
    def _():
        pipeline_step(xin_odd_ref, xin_even_ref)

    @pl.when((e == 0) & (b > 0))
    def _():
        stores = slab_copies(out_hbm, acc_ref, b - 1, False)
        loads = slab_copies(x1_hbm, acc_ref, b, True, sem0=n_slab)
        for cp in stores:
            cp.start()
        for st, ld in zip(stores, loads):
            st.wait()
            ld.start()
        for cp in loads:
            cp.wait()

    @pl.when(step == last)
    def _():
        def drain(j, carry):
            scatter_rows(j * SCATTER_BATCH, step, jnp.float32(1.0))
            return carry

        lax.fori_loop(0, cap // SCATTER_BATCH, drain, 0)
        move(out_hbm, acc_ref, b, False)


def _moe_call(idx_flat, gate_flat, x1, h2, wg, wu, wd, cap):
    B, S, D = x1.shape
    E, _, FF = wg.shape
    assert D % LANES == 0 and cap % SCATTER_BATCH == 0
    n_slab = D // LANES
    smem = pl.BlockSpec(memory_space=pltpu.SMEM)
    hbm = pl.BlockSpec(memory_space=pl.ANY)
    return pl.pallas_call(
        functools.partial(_moe_kernel, cap, S, D),
        grid=(B, E),
        in_specs=[smem, smem, hbm, hbm,
                  pl.BlockSpec((None, D, FF), lambda b, e: (e, 0, 0)),
                  pl.BlockSpec((None, D, FF), lambda b, e: (e, 0, 0)),
                  pl.BlockSpec((None, FF, D), lambda b, e: (e, 0, 0))],
        out_specs=hbm,
        out_shape=jax.ShapeDtypeStruct((B, S, D), F32),
        scratch_shapes=[pltpu.VMEM((n_slab * (S + SLAB_PAD), LANES), F32),
                        pltpu.VMEM((n_slab * (S + SLAB_PAD), LANES), F32),
                        pltpu.VMEM((n_slab * (cap + SLAB_PAD), LANES), F32),
                        pltpu.VMEM((n_slab * (cap + SLAB_PAD), LANES), F32),
                        pltpu.VMEM((n_slab * (cap + SLAB_PAD), LANES), F32),
                        pltpu.SemaphoreType.DMA((2 * n_slab,))],
        compiler_params=_cparams(("arbitrary", "arbitrary")),
        name="moe",
    )(idx_flat, gate_flat, x1, h2, wg, wu, wd)


def _rope_tables(S):
    half = HEAD_DIM // 2
    inv = ROPE_THETA ** (-jnp.arange(half, dtype=F32) / half)
    ang = jnp.arange(S, dtype=F32)[:, None] * inv[None, :]
    cos = jnp.cos(ang)
    sin = jnp.sin(ang)
    heads = LANE_CHUNK // HEAD_DIM
    return (jnp.tile(cos, (1, 2 * heads)),
            jnp.concatenate([jnp.tile(-sin, (1, heads)), jnp.tile(sin, (1, heads))], axis=-1))


def _split_half_columns(w, start, width):
    rows = w.shape[0]
    seg = w[:, start:start + width].reshape(rows, width // QUAD, 4, 2, HEAD_DIM // 2)
    seg = seg.transpose(0, 1, 3, 2, 4).reshape(rows, width)
    return jnp.concatenate([w[:, :start], seg, w[:, start + width:]], axis=1)


def _layer(x, norm1_g, w_in, dil_q_g, dil_k_g, na_q_g, na_k_g, na_rpb, w_dil_branch, w_na_branch,
           w_out, norm2_g, w_router, w_gate, w_up, w_down):
    B, S, D = x.shape
    M = B * S
    tm = 512
    scale = HEAD_DIM ** -0.5
    dil_heads = DIL_WIDTH // HEAD_DIM
    half = HEAD_DIM // 2

    def quad_gains(g, n_heads):
        quad = jnp.concatenate([jnp.tile(g[:half], 4), jnp.tile(g[half:], 4)])
        return jnp.tile(quad, n_heads // 4)

    gain_cols = jnp.concatenate([
        quad_gains(dil_q_g * scale, dil_heads), quad_gains(dil_k_g, dil_heads), jnp.ones((DIL_WIDTH,), F32),
        quad_gains(na_q_g * scale, NA_HEADS), quad_gains(na_k_g, NA_HEADS), jnp.ones((NA_WIDTH + 2 * D,), F32),
    ])[None, :].astype(F32)
    cos_t, sin_t = _rope_tables(S)
    head_of_lane = _split_half_order() // HEAD_DIM
    bd = jnp.asarray(head_of_lane[:, None] == head_of_lane[None, :], BF16)
    w_cols = _split_half_columns(w_in, 0, 2 * DIL_WIDTH)
    w_cols = _split_half_columns(w_cols, 3 * DIL_WIDTH, 2 * NA_WIDTH)

    x2 = x.reshape(M, D)
    proj = _proj_call(x2, norm1_g[None, :], w_cols.astype(BF16), gain_cols, cos_t, sin_t, bd, S, D, tm)
    n_groups = len(DIL_PAIRS)
    qa, ka, va = proj[:n_groups], proj[n_groups:2 * n_groups], proj[2 * n_groups:3 * n_groups]
    qn, kn, vn, ga, gn = proj[3 * n_groups:]

    o_parts, st_parts = [], []
    for g, (window, dilation) in enumerate(DIL_PAIRS):
        o, st = _dilated_call(qa[g], ka[g], va[g], B, S, g, window, dilation)
        o_parts.append(o)
        st_parts.append(st)

    yb = _na_call(qn.reshape(B, S, -1), kn.reshape(B, S, -1), vn.reshape(B, S, -1),
                  _na_bias_table(na_rpb)).reshape(M, NA_WIDTH)

    wr_t = w_router.T
    wr_hi = wr_t.astype(BF16)
    wr_lo = (wr_t - wr_hi.astype(F32)).astype(BF16)
    x1, h2, aff = _mix_call(x2, o_parts, st_parts, yb, ga, gn,
                            w_dil_branch.astype(BF16), w_na_branch.astype(BF16), w_out.astype(BF16),
                            norm2_g[None, :], wr_hi, wr_lo, B, S, MIX_TILE_ROWS)

    cap = (EC_CAPACITY_FACTOR * S) // N_EXPERTS
    idx, gates = _select_call(aff, N_EXPERTS, cap)
    return _moe_call(idx.reshape(-1), gates.reshape(-1),
                     x1.reshape(B, S, D), h2.reshape(B, S, D),
                     w_gate.astype(BF16), w_up.astype(BF16), w_down.astype(BF16), cap)


def kernel(x, norm1_g, w_in, dil_q_norm_g, dil_k_norm_g, na_q_norm_g, na_k_norm_g, na_rpb, w_dil_branch,
           w_na_branch, w_out, norm2_g, w_router, w_gate, w_up, w_down):
    for l in range(norm1_g.shape[0]):
        x = _layer(x, norm1_g[l], w_in[l], dil_q_norm_g[l], dil_k_norm_g[l], na_q_norm_g[l],
                   na_k_norm_g[l], na_rpb[l], w_dil_branch[l], w_na_branch[l], w_out[l], norm2_g[l],
                   w_router[l], w_gate[l], w_up[l], w_down[l])
    return x
```
